```python
import math
import jax, jax.numpy as jnp
from jax import lax
import numpy as np

D_MODEL = 1024
BATCH = 16
SEQ = 4096
DEPTH = 4

GDN_HEADS = 4
GDN_DK = 128
GDN_DV = 128
GDN_CONV = 4
GDN_CHUNK = 64
DIFF_HEADS = 4
DIFF_DH = 64
Q_BLOCK = 128
REL_BUCKETS = 32
REL_MAX_DIST = 128
D_FF = 2816
FFN_CONV = 3
EPS = 1e-6

GDN_QK_W = GDN_HEADS * GDN_DK
GDN_V_W = GDN_HEADS * GDN_DV
DIFF_QK_W = DIFF_HEADS * 2 * DIFF_DH
DIFF_V_W = DIFF_HEADS * 2 * DIFF_DH
MIX_W = GDN_V_W + DIFF_V_W
IN_SPLITS = (GDN_QK_W, GDN_QK_W, GDN_V_W, GDN_V_W, GDN_HEADS, GDN_HEADS, DIFF_QK_W, DIFF_QK_W, DIFF_V_W)
IN_W = sum(IN_SPLITS)
IN_OFFSETS = tuple(int(v) for v in np.cumsum(IN_SPLITS)[:-1])

kernel_name = "hymba_gdn_diffattn_convffn_trunk"

f32 = jnp.float32


def rmsnorm(x, g):
    xf = x.astype(f32)
    y = xf * lax.rsqrt(jnp.mean(xf * xf, axis=-1, keepdims=True) + EPS)
    return (y * g.astype(f32)).astype(x.dtype)


def l2norm(x):
    return x * lax.rsqrt(jnp.sum(x * x, axis=-1, keepdims=True) + EPS)


def causal_dwconv(x, w):
    K, C = w.shape
    return lax.conv_general_dilated(x, w[:, None, :].astype(x.dtype), window_strides=(1,), padding=[(K - 1, 0)],
                                    dimension_numbers=('NWC', 'WIO', 'NWC'), feature_group_count=C)


def t5_causal_bucket(n):
    max_exact = REL_BUCKETS // 2
    nf = jnp.maximum(n, 1).astype(f32)
    large = max_exact + (jnp.log(nf / max_exact) / math.log(REL_MAX_DIST / max_exact)
                         * (REL_BUCKETS - max_exact)).astype(jnp.int32)
    large = jnp.minimum(large, REL_BUCKETS - 1)
    return jnp.where(n < max_exact, n, large)


def chunk_gated_delta_rule(q, k, v, g, beta):
    B, H, T, dk = q.shape
    dv = v.shape[-1]
    C = GDN_CHUNK
    N = T // C
    q = (q * dk ** -0.5).reshape(B, H, N, C, dk)
    k = k.reshape(B, H, N, C, dk)
    v = v.reshape(B, H, N, C, dv)
    beta = beta.reshape(B, H, N, C)[..., None]
    g = jnp.cumsum(g.reshape(B, H, N, C), axis=-1)
    incl = jnp.tril(jnp.ones((C, C), bool))
    strict = jnp.tril(jnp.ones((C, C), bool), -1)
    decay = jnp.exp(jnp.where(incl, g[..., :, None] - g[..., None, :], -jnp.inf))
    k_beta = k * beta
    L = jnp.where(strict, jnp.einsum('bhncd,bhnsd->bhncs', k_beta, k) * decay, 0.0)
    eye = jnp.eye(C, dtype=q.dtype)
    T_inv = lax.linalg.triangular_solve(eye + L, jnp.broadcast_to(eye, L.shape), left_side=True,
                                        lower=True, unit_diagonal=True)
    u = jnp.einsum('bhncs,bhnse->bhnce', T_inv, v * beta)
    w = jnp.einsum('bhncs,bhnsd->bhncd', T_inv, k_beta * jnp.exp(g)[..., None])
    a_intra = jnp.where(incl, jnp.einsum('bhncd,bhnsd->bhncs', q, k) * decay, 0.0)

    def step(S, xs):
        q_c, k_c, u_c, w_c, g_c, a_c = xs
        v_new = u_c - jnp.einsum('bhcd,bhde->bhce', w_c, S)
        o = (jnp.einsum('bhcd,bhde->bhce', q_c * jnp.exp(g_c)[..., None], S)
             + jnp.einsum('bhcs,bhse->bhce', a_c, v_new))
        g_last = g_c[..., -1:]
        S = (S * jnp.exp(g_last)[..., None]
             + jnp.einsum('bhcd,bhce->bhde', k_c * jnp.exp(g_last - g_c)[..., None], v_new))
        return S, o

    xs = tuple(jnp.moveaxis(t, 2, 0) for t in (q, k, u, w, g, a_intra))
    S0 = jnp.zeros((B, H, dk, dv), q.dtype)
    _, o = lax.scan(step, S0, xs)
    return jnp.moveaxis(o, 0, 2).reshape(B, H, T, dv)


def gdn_mixer(q, k, v, gate, b_raw, a_raw, conv_w, a_log, dt_bias, out_norm):
    B, T, _ = q.shape
    qkv = jax.nn.silu(causal_dwconv(jnp.concatenate([q, k, v], axis=-1), conv_w))
    q, k, v = jnp.split(qkv, [GDN_QK_W, 2 * GDN_QK_W], axis=-1)
    heads = lambda t, d: t.reshape(B, T, GDN_HEADS, d).transpose(0, 2, 1, 3).astype(f32)
    q = l2norm(heads(q, GDN_DK))
    k = l2norm(heads(k, GDN_DK))
    v = heads(v, GDN_DV)
    beta = jax.nn.sigmoid(b_raw.astype(f32)).transpose(0, 2, 1)
    g = (-jnp.exp(a_log.astype(f32)) * jax.nn.softplus(a_raw.astype(f32) + dt_bias.astype(f32))).transpose(0, 2, 1)
    o = chunk_gated_delta_rule(q, k, v, g, beta).transpose(0, 2, 1, 3)
    o = rmsnorm(o, out_norm) * jax.nn.silu(gate.reshape(B, T, GDN_HEADS, GDN_DV).astype(f32))
    return o.reshape(B, T, GDN_V_W)


def diff_attn_mixer(q, k, v, q_norm, k_norm, lq1, lk1, lq2, lk2, subln, bias_dist, lambda_init):
    B, T, _ = q.shape
    q = rmsnorm(q.reshape(B, T, DIFF_HEADS, 2, DIFF_DH).astype(f32), q_norm).transpose(0, 2, 3, 1, 4) * DIFF_DH ** -0.5
    k = rmsnorm(k.reshape(B, T, DIFF_HEADS, 2, DIFF_DH).astype(f32), k_norm).transpose(0, 2, 3, 1, 4)
    v = v.reshape(B, T, DIFF_HEADS, 2 * DIFF_DH).astype(f32).transpose(0, 2, 1, 3)
    lam = (jnp.exp(jnp.sum(lq1.astype(f32) * lk1.astype(f32))) - jnp.exp(jnp.sum(lq2.astype(f32) * lk2.astype(f32)))
           + lambda_init)
    kpos = jnp.arange(T)

    def block(start):
        qb = lax.dynamic_slice_in_dim(q, start, Q_BLOCK, axis=3)
        dist = (start + jnp.arange(Q_BLOCK))[:, None] - kpos[None, :]
        bias = bias_dist[:, jnp.clip(dist, 0, T - 1)]
        s = jnp.einsum('bhcqd,bhckd->bhcqk', qb, k) + bias[None, :, None]
        s = jnp.where(dist >= 0, s, -jnp.inf)
        p = jax.nn.softmax(s, axis=-1)
        a = p[:, :, 0] - lam * p[:, :, 1]
        return jnp.einsum('bhqk,bhke->bhqe', a, v)

    o = lax.map(block, jnp.arange(T // Q_BLOCK) * Q_BLOCK)
    o = o.transpose(1, 0, 3, 2, 4).reshape(B, T, DIFF_HEADS, 2 * DIFF_DH)
    o = rmsnorm(o, subln) * (1.0 - lambda_init)
    return o.reshape(B, T, DIFF_V_W)


def conv_ffn(h, w_up, conv_w, conv_b, w_down):
    u = causal_dwconv(h @ w_up, conv_w) + conv_b
    a, g = jnp.split(u, 2, axis=-1)
    return (jax.nn.silu(g) * a) @ w_down


def setup_inputs(seed: int = 0) -> dict:
    key = jax.random.key(seed)
    ks = jax.random.split(key, 26)
    nrm = lambda k, shape, s: jax.random.normal(k, shape, f32) * s
    gain = lambda k, shape: 1.0 + 0.1 * jax.random.normal(k, shape, f32)
    dt = jnp.exp(jax.random.uniform(ks[9], (DEPTH, GDN_HEADS), f32, math.log(1e-3), math.log(1e-1)))
    return {
        "x": nrm(ks[0], (BATCH, SEQ, D_MODEL), 1.0),
        "c": nrm(ks[1], (BATCH, D_MODEL), 1.0),
        "w_ada": nrm(ks[2], (DEPTH, D_MODEL, 6 * D_MODEL), 0.5 * D_MODEL ** -0.5),
        "b_ada": nrm(ks[3], (DEPTH, 6 * D_MODEL), 0.02),
        "norm_mix": gain(ks[4], (DEPTH, D_MODEL)),
        "norm_ffn": gain(ks[5], (DEPTH, D_MODEL)),
        "w_in": nrm(ks[6], (DEPTH, D_MODEL, IN_W), D_MODEL ** -0.5),
        "gdn_conv_w": nrm(ks[7], (DEPTH, GDN_CONV, 2 * GDN_QK_W + GDN_V_W), GDN_CONV ** -0.5),
        "gdn_a_log": jnp.log(jax.random.uniform(ks[8], (DEPTH, GDN_HEADS), f32, 1.0, 16.0)),
        "gdn_dt_bias": dt + jnp.log(-jnp.expm1(-dt)),
        "gdn_out_norm": gain(ks[10], (DEPTH, GDN_DV)),
        "diff_q_norm": gain(ks[11], (DEPTH, DIFF_DH)),
        "diff_k_norm": gain(ks[12], (DEPTH, DIFF_DH)),
        "diff_lambda_q1": nrm(ks[13], (DEPTH, DIFF_DH), 0.1),
        "diff_lambda_k1": nrm(ks[14], (DEPTH, DIFF_DH), 0.1),
        "diff_lambda_q2": nrm(ks[15], (DEPTH, DIFF_DH), 0.1),
        "diff_lambda_k2": nrm(ks[16], (DEPTH, DIFF_DH), 0.1),
        "diff_subln": gain(ks[17], (DEPTH, 2 * DIFF_DH)),
        "rel_bias": nrm(ks[18], (REL_BUCKETS, DIFF_HEADS), 0.5),
        "w_out": nrm(ks[19], (DEPTH, MIX_W, D_MODEL), MIX_W ** -0.5),
        "ffn_up": nrm(ks[20], (DEPTH, D_MODEL, 2 * D_FF), D_MODEL ** -0.5),
        "ffn_conv_w": nrm(ks[21], (DEPTH, FFN_CONV, 2 * D_FF), FFN_CONV ** -0.5),
        "ffn_conv_b": nrm(ks[22], (DEPTH, 2 * D_FF), 0.02),
        "ffn_down": nrm(ks[23], (DEPTH, D_FF, D_MODEL), D_FF ** -0.5),
    }


def reference(x, c, w_ada, b_ada, norm_mix, norm_ffn, w_in, gdn_conv_w, gdn_a_log, gdn_dt_bias, gdn_out_norm,
              diff_q_norm, diff_k_norm, diff_lambda_q1, diff_lambda_k1, diff_lambda_q2, diff_lambda_k2, diff_subln,
              rel_bias, w_out, ffn_up, ffn_conv_w, ffn_conv_b, ffn_down):
    T = x.shape[1]
    bias_dist = rel_bias.astype(f32)[t5_causal_bucket(jnp.arange(T, dtype=jnp.int32))].T
    c_act = jax.nn.silu(c)
    for l in range(DEPTH):
        mod = (c_act @ w_ada[l] + b_ada[l])[:, None, :]
        sh_a, sc_a, gt_a, sh_m, sc_m, gt_m = jnp.split(mod, 6, axis=-1)
        h = rmsnorm(x, norm_mix[l]) * (1.0 + sc_a) + sh_a
        gq, gk, gv, ggate, gb, ga, dq, dk, dv = jnp.split(h @ w_in[l], IN_OFFSETS, axis=-1)
        y_gdn = gdn_mixer(gq, gk, gv, ggate, gb, ga, gdn_conv_w[l], gdn_a_log[l], gdn_dt_bias[l], gdn_out_norm[l])
        lambda_init = 0.8 - 0.6 * math.exp(-0.3 * l)
        y_diff = diff_attn_mixer(dq, dk, dv, diff_q_norm[l], diff_k_norm[l], diff_lambda_q1[l], diff_lambda_k1[l],
                                 diff_lambda_q2[l], diff_lambda_k2[l], diff_subln[l], bias_dist, lambda_init)
        y = jnp.concatenate([y_gdn, y_diff], axis=-1).astype(x.dtype) @ w_out[l]
        x = x + gt_a * y
        h = rmsnorm(x, norm_ffn[l]) * (1.0 + sc_m) + sh_m
        x = x + gt_m * conv_ffn(h, ffn_up[l], ffn_conv_w[l], ffn_conv_b[l], ffn_down[l])
    return x
```

```python
import functools
import math

import numpy as np
import jax
import jax.numpy as jnp
from jax import lax
from jax.experimental import pallas as pl
from jax.experimental.pallas import tpu as pltpu

f32 = jnp.float32
bf16 = jnp.bfloat16

EPS = 1e-6
GDN_HEADS = 4
GDN_DK = 128
GDN_CONV = 4
GDN_CHUNK = 64
DIFF_HEADS = 4
DIFF_DH = 64
REL_BUCKETS = 32
REL_MAX_DIST = 128
FFN_CONV = 3
HEAD_W = 128
GROUP_W = GDN_HEADS * HEAD_W
NEG_BIG = -1e30

VMEM_LIMIT_BYTES = 56 * 1024 * 1024

COL_GQ, COL_GK, COL_GV, COL_GATE, COL_DQ, COL_DK, COL_DV, COL_BA = (
    0, GROUP_W, 2 * GROUP_W, 3 * GROUP_W, 4 * GROUP_W, 5 * GROUP_W, 6 * GROUP_W, 7 * GROUP_W)
PROJ_W = 7 * GROUP_W + HEAD_W

TOKEN_TILE = 512
ATTN_TQ = 256
ATTN_TK = 256
FFN_CHUNK = 256


def _bdot(a, b):
    return jnp.dot(a.astype(bf16), b.astype(bf16), preferred_element_type=f32)


def _bdot_nt(a, b):
    return lax.dot_general(a.astype(bf16), b.astype(bf16), (((1,), (1,)), ((), ())),
                           preferred_element_type=f32)


def _silu(x):
    return x * (1.0 / (1.0 + jnp.exp(-x)))


def _params(semantics):
    return pltpu.CompilerParams(dimension_semantics=semantics, vmem_limit_bytes=VMEM_LIMIT_BYTES)


def _const_spec(shape):
    nd = len(shape)
    return pl.BlockSpec(shape, lambda *_: (0,) * nd, pipeline_mode=pl.Buffered(1))


def _ada_kernel(c_ref, w_ref, b_ref, o_ref):
    ca = _silu(c_ref[...])
    o_ref[0] = _bdot(ca, w_ref[0]) + b_ref[0]


def _ada_modulation(c, w_ada, b_ada):
    L, D, N = w_ada.shape
    B = c.shape[0]
    tn = 1536
    return pl.pallas_call(
        _ada_kernel,
        out_shape=jax.ShapeDtypeStruct((L, B, N), f32),
        grid=(L, N // tn),
        in_specs=[
            pl.BlockSpec((B, D), lambda l, n: (0, 0)),
            pl.BlockSpec((1, D, tn), lambda l, n: (l, 0, n)),
            pl.BlockSpec((1, 1, tn), lambda l, n: (l, 0, n)),
        ],
        out_specs=pl.BlockSpec((1, B, tn), lambda l, n: (l, 0, n)),
        compiler_params=_params(("arbitrary", "arbitrary")),
        name="ada_modulation",
    )(c, w_ada, b_ada.reshape(L, 1, N))


def _bucket_of_distance(n):
    max_exact = REL_BUCKETS // 2
    nf = np.maximum(n, 1).astype(np.float32)
    large = max_exact + (np.log(nf / np.float32(max_exact)) / np.float32(math.log(REL_MAX_DIST / max_exact))
                         * np.float32(REL_BUCKETS - max_exact)).astype(np.int32)
    large = np.minimum(large, REL_BUCKETS - 1)
    return np.where(n < max_exact, n, large).astype(np.int32)


def _bias_bucket_tiles(tq, tk):
    i = np.arange(tq)[:, None]
    j = np.arange(tk)[None, :]
    d0 = i - j
    diag = np.where(d0 >= 0, _bucket_of_distance(np.maximum(d0, 0)), -1)
    sub = _bucket_of_distance(tk + d0)
    assert _bucket_of_distance(np.array([tk + 1]))[0] == REL_BUCKETS - 1
    return np.stack([diag, sub]).astype(np.int32)


def _bias_tile_kernel(rel_ref, bk_ref, o_ref):
    h = pl.program_id(1)
    bk = bk_ref[0]
    last = rel_ref[REL_BUCKETS - 1, h]
    acc = jnp.where(bk < 0, NEG_BIG, 0.0).astype(f32)
    for b in range(REL_BUCKETS - 1):
        acc = jnp.where(bk == b, rel_ref[b, h] - last, acc)
    o_ref[0, 0] = acc


def _bias_tiles(rel_bias, tq, tk):
    buckets = jnp.asarray(_bias_bucket_tiles(tq, tk))
    return pl.pallas_call(
        _bias_tile_kernel,
        out_shape=jax.ShapeDtypeStruct((2, DIFF_HEADS, tq, tk), f32),
        grid=(2, DIFF_HEADS),
        in_specs=[
            pl.BlockSpec(memory_space=pltpu.SMEM),
            pl.BlockSpec((1, tq, tk), lambda k, h: (k, 0, 0)),
        ],
        out_specs=pl.BlockSpec((1, 1, tq, tk), lambda k, h: (k, h, 0, 0)),
        compiler_params=_params(("arbitrary", "arbitrary")),
        name="rel_bias_tiles",
    )(rel_bias.astype(f32), buckets)


def _modulated_norm(x, gain_row, mod_ref, shift_idx, scale_idx):
    ms = jnp.mean(x * x, axis=-1, keepdims=True)
    y = x * lax.rsqrt(ms + EPS)
    return y * (gain_row * (1.0 + mod_ref[0, scale_idx:scale_idx + 1, :])) + mod_ref[0, shift_idx:shift_idx + 1, :]


def _inproj_kernel(x_ref, mod_ref, g_ref, w_ref, o_ref, h_scr):
    h_scr[...] = _modulated_norm(x_ref[0], g_ref[...], mod_ref, 0, 1).astype(bf16)
    n = o_ref.shape[2]
    for c0 in range(0, n, GROUP_W):
        c1 = min(c0 + GROUP_W, n)
        o_ref[0, :, c0:c1] = jnp.dot(h_scr[...], w_ref[:, c0:c1], preferred_element_type=f32)


def _in_projection(x, mod_l, gain, w_perm):
    B, T, D = x.shape
    tm = TOKEN_TILE
    return pl.pallas_call(
        _inproj_kernel,
        out_shape=jax.ShapeDtypeStruct((B, T, PROJ_W), f32),
        grid=(B, T // tm),
        in_specs=[
            pl.BlockSpec((1, tm, D), lambda b, t: (b, t, 0)),
            pl.BlockSpec((1, 6, D), lambda b, t: (b, 0, 0)),
            _const_spec((1, D)),
            _const_spec((D, PROJ_W)),
        ],
        out_specs=pl.BlockSpec((1, tm, PROJ_W), lambda b, t: (b, t, 0)),
        scratch_shapes=[pltpu.VMEM((tm, D), bf16)],
        compiler_params=_params(("arbitrary", "arbitrary")),
        name="in_projection",
    )(x, mod_l, gain.reshape(1, D), w_perm)


def _shift_rows(x, prev8, s):
    xs = pltpu.roll(x, s, axis=0)
    fix = pltpu.roll(prev8, s, axis=0)
    row = lax.broadcasted_iota(jnp.int32, fix.shape, 0)
    head = jnp.where(row < s, fix, xs[:8])
    return jnp.concatenate([head, xs[8:]], axis=0)


def _causal_conv(x, prev8, w_ref, taps):
    acc = x * w_ref[taps - 1:taps, :]
    for s in range(1, taps):
        acc = acc + _shift_rows(x, prev8, s) * w_ref[taps - 1 - s:taps - s, :]
    return acc


def _block_lower_inverse(low, base, top):
    n = low.shape[0]
    row = lax.broadcasted_iota(jnp.int32, (n, n), 0)
    col = lax.broadcasted_iota(jnp.int32, (n, n), 1)
    same_block = lambda size: (row // size) == (col // size)
    lb = jnp.where(same_block(base), low, 0.0)
    x = jnp.where(row == col, 1.0, 0.0).astype(f32) - lb
    p = _bdot(lb, lb)
    for _ in range(int(math.log2(base)) - 2):
        xp = _bdot(jnp.concatenate([x, p], axis=0), p)
        x = x + xp[:n]
        p = xp[n:]
    x = x + _bdot(x, p)
    size = base
    while size < top:
        off = jnp.where(jnp.logical_and(same_block(2 * size), jnp.logical_not(same_block(size))), low, 0.0)
        x = x - _bdot(_bdot(x, off), x)
        size *= 2
    return x


def _gdn_kernel(qkv_ref, gate_ref, ba_ref, convw_ref, alog_ref, dt_ref, onorm_ref, o_ref,
                q_s, k_s, v_s, gc_s, beta_s, state_s, carry_s):
    tg = qkv_ref.shape[1]
    C = GDN_CHUNK

    @pl.when(pl.program_id(1) == 0)
    def _():
        state_s[...] = jnp.zeros_like(state_s)
        carry_s[...] = jnp.zeros_like(carry_s)

    for grp, dst in enumerate((q_s, k_s, v_s)):
        for h in range(GDN_HEADS):
            c0 = grp * GROUP_W + h * HEAD_W
            x = qkv_ref[0, :, c0:c0 + HEAD_W]
            y = _silu(_causal_conv(x, carry_s[:, c0:c0 + HEAD_W], convw_ref.at[:, c0:c0 + HEAD_W], GDN_CONV))
            if grp < 2:
                y = y * lax.rsqrt(jnp.sum(y * y, axis=-1, keepdims=True) + EPS)
            if grp == 0:
                y = y * (GDN_DK ** -0.5)
            dst[:, h * HEAD_W:(h + 1) * HEAD_W] = y
    carry_s[...] = qkv_ref[0, tg - 8:tg, :]

    ba = ba_ref[0]
    beta_s[...] = 1.0 / (1.0 + jnp.exp(-ba))
    z = ba + dt_ref[...]
    g = -jnp.exp(alog_ref[...]) * (jnp.maximum(z, 0.0) + jnp.log1p(jnp.exp(-jnp.abs(z))))
    rowc = lax.broadcasted_iota(jnp.int32, g.shape, 0) % C
    s = 1
    while s < C:
        g = g + jnp.where(rowc >= s, pltpu.roll(g, s, axis=0), 0.0)
        s *= 2
    gc_s[...] = g

    H = GDN_HEADS
    n = H * C
    ri = lax.broadcasted_iota(jnp.int32, (n, n), 0)
    ci = lax.broadcasted_iota(jnp.int32, (n, n), 1)
    same_head = (ri // C) == (ci // C)
    incl = jnp.logical_and(same_head, ri >= ci)
    strict = jnp.logical_and(same_head, ri > ci)
    onorm = onorm_ref[...]

    def chunk_body(c, carry):
        r0 = pl.multiple_of(c * C, C)
        rows = pl.ds(r0, C)
        gcc = gc_s[rows, :]
        gct = gcc.T
        betac = beta_s[rows, :]
        glast = gc_s[pl.ds(r0 + C - 1, 1), :]
        ks, kbs, qs, rhs, qgs, kdecs, gcols, grows, egls = [], [], [], [], [], [], [], [], []
        for h in range(H):
            cols = slice(h * HEAD_W, (h + 1) * HEAD_W)
            q = q_s[rows, cols]
            k = k_s[rows, cols]
            v = v_s[rows, cols]
            beta = betac[:, h:h + 1]
            gcol = gcc[:, H + h:H + h + 1]
            gl = glast[:, H + h:H + h + 1]
            eg = jnp.exp(gcol)
            kb = k * beta
            ks.append(k)
            kbs.append(kb)
            qs.append(q)
            rhs.append(jnp.concatenate([v * beta, kb * eg], axis=1))
            qgs.append(q * eg)
            kdecs.append(k * jnp.exp(gl - gcol))
            gcols.append(gcol)
            grows.append(gct[H + h:H + h + 1, :])
            egls.append(jnp.exp(gl))
        gcol_all = jnp.concatenate(gcols, axis=0)
        grow_all = jnp.concatenate(grows, axis=1)
        decay = jnp.exp(jnp.where(incl, gcol_all - grow_all, NEG_BIG))
        kq = _bdot_nt(jnp.concatenate(kbs + qs, axis=0), jnp.concatenate(ks, axis=0))
        low = jnp.where(strict, kq[:n] * decay, 0.0)
        a_intra = kq[n:] * decay
        t_inv = _block_lower_inverse(low, 16, C)
        uw = _bdot(t_inv, jnp.concatenate(rhs, axis=0))
        v_news, q_states = [], []
        for h in range(H):
            hr = slice(h * C, (h + 1) * C)
            state = state_s[h]
            ws = _bdot(jnp.concatenate([uw[hr, HEAD_W:], qgs[h]], axis=0), state)
            v_new = uw[hr, :HEAD_W] - ws[:C]
            v_news.append(v_new)
            q_states.append(ws[C:])
            state_s[h] = state * egls[h] + _bdot(kdecs[h].T, v_new)
        o_all = jnp.concatenate(q_states, axis=0) + _bdot(a_intra, jnp.concatenate(v_news, axis=0))
        for h in range(H):
            cols = slice(h * HEAD_W, (h + 1) * HEAD_W)
            o = o_all[h * C:(h + 1) * C]
            on = o * lax.rsqrt(jnp.mean(o * o, axis=-1, keepdims=True) + EPS) * onorm
            o_ref[0, rows, cols] = on * _silu(gate_ref[0, rows, cols])
        return carry

    lax.fori_loop(0, tg // C, chunk_body, 0)


def _gdn_mixer(proj, conv_w, a_log, dt_bias, out_norm):
    B, T, _ = proj.shape
    tg = TOKEN_TILE
    convw = jnp.zeros((8, 3 * GROUP_W), f32).at[:GDN_CONV].set(conv_w.astype(f32))
    lane_pad = lambda v: jnp.zeros((1, HEAD_W), f32).at[0, GDN_HEADS:2 * GDN_HEADS].set(v.astype(f32))
    return pl.pallas_call(
        _gdn_kernel,
        out_shape=jax.ShapeDtypeStruct((B, T, GROUP_W), f32),
        grid=(B, T // tg),
        in_specs=[
            pl.BlockSpec((1, tg, 3 * GROUP_W), lambda b, t: (b, t, 0)),
            pl.BlockSpec((1, tg, GROUP_W), lambda b, t: (b, t, COL_GATE // GROUP_W)),
            pl.BlockSpec((1, tg, HEAD_W), lambda b, t: (b, t, COL_BA // HEAD_W)),
            _const_spec((8, 3 * GROUP_W)),
            _const_spec((1, HEAD_W)),
            _const_spec((1, HEAD_W)),
            _const_spec((1, HEAD_W)),
        ],
        out_specs=pl.BlockSpec((1, tg, GROUP_W), lambda b, t: (b, t, 0)),
        scratch_shapes=[
            pltpu.VMEM((tg, GROUP_W), f32),
            pltpu.VMEM((tg, GROUP_W), f32),
            pltpu.VMEM((tg, GROUP_W), f32),
            pltpu.VMEM((tg, HEAD_W), f32),
            pltpu.VMEM((tg, HEAD_W), f32),
            pltpu.VMEM((GDN_HEADS, GDN_DK, HEAD_W), f32),
            pltpu.VMEM((8, 3 * GROUP_W), f32),
        ],
        compiler_params=_params(("arbitrary", "arbitrary")),
        name="gdn_mixer",
    )(proj, proj, proj, convw, lane_pad(a_log), lane_pad(dt_bias), out_norm.reshape(1, HEAD_W).astype(f32))


def _half_rmsnorm(x, gain_row):
    lane = lax.broadcasted_iota(jnp.int32, x.shape, 1)
    lo = lane < DIFF_DH
    x2 = x * x
    s_lo = jnp.sum(jnp.where(lo, x2, 0.0), axis=-1, keepdims=True)
    s_hi = jnp.sum(jnp.where(lo, 0.0, x2), axis=-1, keepdims=True)
    ms = jnp.where(lo, s_lo, s_hi) * (1.0 / DIFF_DH)
    return x * lax.rsqrt(ms + EPS) * gain_row


def _attn_kernel(q_ref, k_ref, v_ref, db_ref, qn_ref, kn_ref, lam_ref, subln_ref, o_ref,
                 k_s, v_s, m_s, l_s, acc_s, *, lambda_init):
    tq = q_ref.shape[1]
    tk = db_ref.shape[3]
    T = k_ref.shape[1]
    qi = pl.program_id(2)

    @pl.when(qi == 0)
    def _():
        def prep(i, carry):
            rows = pl.ds(pl.multiple_of(i * tk, tk), tk)
            k_s[rows, :] = _half_rmsnorm(k_ref[0, rows, :], kn_ref[...]).astype(bf16)
            v_s[rows, :] = v_ref[0, rows, :].astype(bf16)
            return carry
        lax.fori_loop(0, T // tk, prep, 0)

    qn = _half_rmsnorm(q_ref[0], qn_ref[...]) * (DIFF_DH ** -0.5)
    lane = lax.broadcasted_iota(jnp.int32, qn.shape, 1)
    qq = jnp.concatenate([jnp.where(lane < DIFF_DH, qn, 0.0), jnp.where(lane < DIFF_DH, 0.0, qn)],
                         axis=0).astype(bf16)

    m_s[...] = jnp.full_like(m_s, NEG_BIG)
    l_s[...] = jnp.zeros_like(l_s)
    acc_s[...] = jnp.zeros_like(acc_s)

    def tile(j, bias):
        rows = pl.ds(pl.multiple_of(j * tk, tk), tk)
        s = lax.dot_general(qq, k_s[rows, :], (((1,), (1,)), ((), ())), preferred_element_type=f32)
        if bias is not None:
            s = s + jnp.concatenate([bias, bias], axis=0)
        m_old = m_s[...]
        m_new = jnp.maximum(m_old, jnp.max(s, axis=-1, keepdims=True))
        alpha = jnp.exp(m_old - m_new)
        p = jnp.exp(s - m_new)
        l_s[...] = alpha * l_s[...] + jnp.sum(p, axis=-1, keepdims=True)
        acc_s[...] = alpha * acc_s[...] + jnp.dot(p.astype(bf16), v_s[rows, :], preferred_element_type=f32)
        m_s[...] = m_new

    tile(qi, db_ref[0, 0])

    @pl.when(qi >= 1)
    def _():
        tile(qi - 1, db_ref[1, 0])

    def far(j, carry):
        tile(j, None)
        return carry
    lax.fori_loop(0, jnp.maximum(qi - 1, 0), far, 0)

    lam_rows = lam_ref[...]
    lam = (jnp.exp(jnp.sum(lam_rows[0:1] * lam_rows[1:2], axis=-1, keepdims=True))
           - jnp.exp(jnp.sum(lam_rows[2:3] * lam_rows[3:4], axis=-1, keepdims=True)) + lambda_init)
    l = l_s[...]
    acc = acc_s[...]
    o = acc[:tq] / l[:tq] - lam * (acc[tq:] / l[tq:])
    on = o * lax.rsqrt(jnp.mean(o * o, axis=-1, keepdims=True) + EPS) * subln_ref[...]
    o_ref[0] = on * (1.0 - lambda_init)


def _diff_attention(proj, db, q_norm, k_norm, lq1, lk1, lq2, lk2, subln, lambda_init):
    B, T, _ = proj.shape
    tq, tk = ATTN_TQ, ATTN_TK
    two = lambda v: jnp.concatenate([v, v]).reshape(1, HEAD_W).astype(f32)
    lam_rows = jnp.zeros((8, DIFF_DH), f32).at[:4].set(jnp.stack([lq1, lk1, lq2, lk2]).astype(f32))
    qb, kb, vb = COL_DQ // HEAD_W, COL_DK // HEAD_W, COL_DV // HEAD_W
    return pl.pallas_call(
        functools.partial(_attn_kernel, lambda_init=lambda_init),
        out_shape=jax.ShapeDtypeStruct((B, T, GROUP_W), f32),
        grid=(B, DIFF_HEADS, T // tq),
        in_specs=[
            pl.BlockSpec((1, tq, HEAD_W), lambda b, h, i: (b, i, qb + h)),
            pl.BlockSpec((1, T, HEAD_W), lambda b, h, i: (b, 0, kb + h)),
            pl.BlockSpec((1, T, HEAD_W), lambda b, h, i: (b, 0, vb + h)),
            pl.BlockSpec((2, 1, tq, tk), lambda b, h, i: (0, h, 0, 0)),
            _const_spec((1, HEAD_W)),
            _const_spec((1, HEAD_W)),
            _const_spec((8, DIFF_DH)),
            _const_spec((1, HEAD_W)),
        ],
        out_specs=pl.BlockSpec((1, tq, HEAD_W), lambda b, h, i: (b, i, h)),
        scratch_shapes=[
            pltpu.VMEM((T, HEAD_W), bf16),
            pltpu.VMEM((T, HEAD_W), bf16),
            pltpu.VMEM((2 * tq, 1), f32),
            pltpu.VMEM((2 * tq, 1), f32),
            pltpu.VMEM((2 * tq, HEAD_W), f32),
        ],
        compiler_params=_params(("arbitrary", "arbitrary", "arbitrary")),
        name="diff_attention",
    )(proj, proj, proj, db, two(q_norm), two(k_norm), lam_rows, subln.reshape(1, HEAD_W).astype(f32))


def _outffn_kernel(x_ref, yg_ref, yd_ref, mod_ref, g_ref, wout_ref, wup_ref, cw_ref, wdown_ref, o_ref,
                   h_s, acc_s, carry_s):
    nchunks = wup_ref.shape[0]
    fc = wdown_ref.shape[1]

    @pl.when(pl.program_id(1) == 0)
    def _():
        carry_s[...] = jnp.zeros_like(carry_s)

    half = yg_ref.shape[2]
    y = (jnp.dot(yg_ref[0].astype(bf16), wout_ref[:half, :], preferred_element_type=f32)
         + jnp.dot(yd_ref[0].astype(bf16), wout_ref[half:, :], preferred_element_type=f32))
    x1 = x_ref[0] + mod_ref[0, 2:3, :] * y
    o_ref[0] = x1
    h_s[...] = _modulated_norm(x1, g_ref[...], mod_ref, 3, 4).astype(bf16)
    acc_s[...] = jnp.zeros_like(acc_s)
    tm = h_s.shape[0]

    def chunk(j, carry):
        u = jnp.dot(h_s[...], wup_ref[j], preferred_element_type=f32)
        cw = cw_ref.at[j]
        yc = _causal_conv(u, carry_s[j], cw, FFN_CONV) + cw[FFN_CONV:FFN_CONV + 1, :]
        carry_s[j] = u[tm - 8:tm, :]
        act = _silu(yc[:, fc:]) * yc[:, :fc]
        acc_s[...] += jnp.dot(act.astype(bf16), wdown_ref[j], preferred_element_type=f32)
        return carry

    lax.fori_loop(0, nchunks, chunk, 0)
    o_ref[0] = o_ref[0] + mod_ref[0, 5:6, :] * acc_s[...]


def _out_ffn(x, y_gdn, y_diff, mod_l, gain, w_out, w_up_c, conv_c, w_down_c):
    B, T, D = x.shape
    tm = TOKEN_TILE
    nchunks, _, fc2 = w_up_c.shape
    fc = fc2 // 2
    return pl.pallas_call(
        _outffn_kernel,
        out_shape=jax.ShapeDtypeStruct((B, T, D), f32),
        grid=(B, T // tm),
        in_specs=[
            pl.BlockSpec((1, tm, D), lambda b, t: (b, t, 0)),
            pl.BlockSpec((1, tm, GROUP_W), lambda b, t: (b, t, 0)),
            pl.BlockSpec((1, tm, GROUP_W), lambda b, t: (b, t, 0)),
            pl.BlockSpec((1, 6, D), lambda b, t: (b, 0, 0)),
            _const_spec((1, D)),
            _const_spec((2 * GROUP_W, D)),
            _const_spec((nchunks, D, fc2)),
            _const_spec((nchunks, 8, fc2)),
            _const_spec((nchunks, fc, D)),
        ],
        out_specs=pl.BlockSpec((1, tm, D), lambda b, t: (b, t, 0)),
        scratch_shapes=[
            pltpu.VMEM((tm, D), bf16),
            pltpu.VMEM((tm, D), f32),
            pltpu.VMEM((nchunks, 8, fc2), f32),
        ],
        compiler_params=_params(("arbitrary", "arbitrary")),
        name="out_ffn",
    )(x, y_gdn, y_diff, mod_l, gain.reshape(1, D), w_out, w_up_c, conv_c, w_down_c)


def _regroup_w_in(w_in_l):
    D = w_in_l.shape[0]
    o_b = 4 * GROUP_W
    o_d = o_b + 2 * GDN_HEADS
    small = jnp.zeros((D, HEAD_W), w_in_l.dtype).at[:, :2 * GDN_HEADS].set(w_in_l[:, o_b:o_d])
    return jnp.concatenate([w_in_l[:, :o_b], w_in_l[:, o_d:], small], axis=1).astype(bf16)


def _chunk_ffn(ffn_up_l, conv_w_l, conv_b_l, ffn_down_l, fc):
    D, two_f = ffn_up_l.shape
    F = two_f // 2
    n = F // fc
    pair = lambda m: jnp.concatenate([m[..., :F].reshape(m.shape[:-1] + (n, fc)),
                                      m[..., F:].reshape(m.shape[:-1] + (n, fc))], axis=-1)
    w_up_c = jnp.moveaxis(pair(ffn_up_l), -2, 0).astype(bf16)
    conv_rows = jnp.concatenate([conv_w_l.astype(f32), conv_b_l.astype(f32)[None]], axis=0)
    conv_c = jnp.moveaxis(pair(conv_rows), -2, 0)
    conv_c = jnp.zeros((n, 8, 2 * fc), f32).at[:, :FFN_CONV + 1].set(conv_c)
    w_down_c = ffn_down_l.reshape(n, fc, ffn_down_l.shape[1]).astype(bf16)
    return w_up_c, conv_c, w_down_c


def kernel(x, c, w_ada, b_ada, norm_mix, norm_ffn, w_in, gdn_conv_w, gdn_a_log, gdn_dt_bias, gdn_out_norm,
           diff_q_norm, diff_k_norm, diff_lambda_q1, diff_lambda_k1, diff_lambda_q2, diff_lambda_k2, diff_subln,
           rel_bias, w_out, ffn_up, ffn_conv_w, ffn_conv_b, ffn_down):
    B, T, D = x.shape
    depth = w_in.shape[0]
    mod = _ada_modulation(c, w_ada, b_ada).reshape(depth, B, 6, D)
    db = _bias_tiles(rel_bias, ATTN_TQ, ATTN_TK)
    for l in range(depth):
        proj = _in_projection(x, mod[l], norm_mix[l], _regroup_w_in(w_in[l]))
        y_gdn = _gdn_mixer(proj, gdn_conv_w[l], gdn_a_log[l], gdn_dt_bias[l], gdn_out_norm[l])
        lambda_init = 0.8 - 0.6 * math.exp(-0.3 * l)
        y_diff = _diff_attention(proj, db, diff_q_norm[l], diff_k_norm[l], diff_lambda_q1[l], diff_lambda_k1[l],
                                 diff_lambda_q2[l], diff_lambda_k2[l], diff_subln[l], lambda_init)
        w_up_c, conv_c, w_down_c = _chunk_ffn(ffn_up[l], ffn_conv_w[l], ffn_conv_b[l], ffn_down[l], FFN_CHUNK)
        x = _out_ffn(x, y_gdn, y_diff, mod[l], norm_ffn[l], w_out[l].astype(bf16), w_up_c, conv_c, w_down_c)
    return x
```

```python
import functools
import math

import numpy as np
import jax
import jax.numpy as jnp
from jax import lax
from jax.experimental import pallas as pl
from jax.experimental.pallas import tpu as pltpu

f32 = jnp.float32
bf16 = jnp.bfloat16

EPS = 1e-6
GDN_HEADS = 4
GDN_DK = 128
GDN_CONV = 4
GDN_CHUNK = 64
DIFF_HEADS = 4
DIFF_DH = 64
REL_BUCKETS = 32
REL_MAX_DIST = 128
FFN_CONV = 3
HEAD_W = 128
GROUP_W = GDN_HEADS * HEAD_W
NEG_BIG = -1e30
LOG2E = math.log2(math.e)
BF16_ROWS = 16

VMEM_LIMIT_BYTES = 56 * 1024 * 1024

COL_GQ, COL_GK, COL_GV, COL_GATE, COL_DQ, COL_DK, COL_DV, COL_BA = (
    0, GROUP_W, 2 * GROUP_W, 3 * GROUP_W, 4 * GROUP_W, 5 * GROUP_W, 6 * GROUP_W, 7 * GROUP_W)
PROJ_W = 7 * GROUP_W + HEAD_W

TOKEN_TILE = 512
ATTN_TQ = 256
ATTN_TK = 256
ATTN_GROUP = 4
ATTN_LOOKAHEAD = 4
STALE_SHIFT_MARGIN = 64.0
FFN_CHUNK = 256
FFN_ROW_SPLIT = 2


def _bdot(a, b):
    return jnp.dot(a.astype(bf16), b.astype(bf16), preferred_element_type=f32)


def _bdot_nt(a, b):
    return lax.dot_general(a.astype(bf16), b.astype(bf16), (((1,), (1,)), ((), ())),
                           preferred_element_type=f32)


def _silu(x):
    return x * (1.0 / (1.0 + jnp.exp(-x)))


def _params(semantics):
    return pltpu.CompilerParams(dimension_semantics=semantics, vmem_limit_bytes=VMEM_LIMIT_BYTES)


def _const_spec(shape):
    nd = len(shape)
    return pl.BlockSpec(shape, lambda *_: (0,) * nd, pipeline_mode=pl.Buffered(1))


def _ada_kernel(c_ref, w_ref, b_ref, o_ref):
    ca = _silu(c_ref[...])
    o_ref[0] = _bdot(ca, w_ref[0]) + b_ref[0]


def _ada_modulation(c, w_ada, b_ada):
    L, D, N = w_ada.shape
    B = c.shape[0]
    tn = 1536
    return pl.pallas_call(
        _ada_kernel,
        out_shape=jax.ShapeDtypeStruct((L, B, N), f32),
        grid=(L, N // tn),
        in_specs=[
            pl.BlockSpec((B, D), lambda l, n: (0, 0)),
            pl.BlockSpec((1, D, tn), lambda l, n: (l, 0, n)),
            pl.BlockSpec((1, 1, tn), lambda l, n: (l, 0, n)),
        ],
        out_specs=pl.BlockSpec((1, B, tn), lambda l, n: (l, 0, n)),
        compiler_params=_params(("arbitrary", "arbitrary")),
        name="ada_modulation",
    )(c, w_ada, b_ada.reshape(L, 1, N))


def _bucket_of_distance(n):
    max_exact = REL_BUCKETS // 2
    nf = np.maximum(n, 1).astype(np.float32)
    large = max_exact + (np.log(nf / np.float32(max_exact)) / np.float32(math.log(REL_MAX_DIST / max_exact))
                         * np.float32(REL_BUCKETS - max_exact)).astype(np.int32)
    large = np.minimum(large, REL_BUCKETS - 1)
    return np.where(n < max_exact, n, large).astype(np.int32)


def _bias_bucket_tiles(tq, tk):
    assert tq == tk
    i = np.arange(tq)[None, :]
    j = np.arange(tk)[:, None]
    d0 = i - j
    diag = np.where(d0 >= 0, _bucket_of_distance(np.maximum(d0, 0)), -1)
    sub = _bucket_of_distance(tk + d0)
    assert _bucket_of_distance(np.array([tk + 1]))[0] == REL_BUCKETS - 1
    both = np.stack([diag, sub]).astype(np.int32)
    return np.concatenate([both, both], axis=2)


def _bias_tile_kernel(rel_ref, bk_ref, o_ref):
    h = pl.program_id(1)
    bk = bk_ref[0]
    last = rel_ref[REL_BUCKETS - 1, h]
    acc = jnp.where(bk < 0, NEG_BIG, 0.0).astype(f32)
    for b in range(REL_BUCKETS - 1):
        acc = jnp.where(bk == b, (rel_ref[b, h] - last) * LOG2E, acc)
    o_ref[0, 0] = acc


def _bias_tiles(rel_bias, tq, tk):
    buckets = jnp.asarray(_bias_bucket_tiles(tq, tk))
    return pl.pallas_call(
        _bias_tile_kernel,
        out_shape=jax.ShapeDtypeStruct((2, DIFF_HEADS, tk, 2 * tq), f32),
        grid=(2, DIFF_HEADS),
        in_specs=[
            pl.BlockSpec(memory_space=pltpu.SMEM),
            pl.BlockSpec((1, tk, 2 * tq), lambda k, h: (k, 0, 0)),
        ],
        out_specs=pl.BlockSpec((1, 1, tk, 2 * tq), lambda k, h: (k, h, 0, 0)),
        compiler_params=_params(("arbitrary", "arbitrary")),
        name="rel_bias_tiles",
    )(rel_bias.astype(f32), buckets)


def _modulated_norm(x, gain_row, mod_ref, shift_idx, scale_idx):
    ms = jnp.mean(x * x, axis=-1, keepdims=True)
    y = x * lax.rsqrt(ms + EPS)
    return y * (gain_row * (1.0 + mod_ref[0, scale_idx:scale_idx + 1, :])) + mod_ref[0, shift_idx:shift_idx + 1, :]


def _inproj_kernel(x_ref, mod_ref, g_ref, w_ref, o_ref, h_scr):
    h_scr[...] = _modulated_norm(x_ref[0], g_ref[...], mod_ref, 0, 1).astype(bf16)
    n = o_ref.shape[2]
    for c0 in range(0, n, GROUP_W):
        c1 = min(c0 + GROUP_W, n)
        o_ref[0, :, c0:c1] = jnp.dot(h_scr[...], w_ref[:, c0:c1], preferred_element_type=f32)


def _in_projection(x, mod_l, gain, w_perm):
    B, T, D = x.shape
    tm = TOKEN_TILE
    return pl.pallas_call(
        _inproj_kernel,
        out_shape=jax.ShapeDtypeStruct((B, T, PROJ_W), f32),
        grid=(B, T // tm),
        in_specs=[
            pl.BlockSpec((1, tm, D), lambda b, t: (b, t, 0)),
            pl.BlockSpec((1, 6, D), lambda b, t: (b, 0, 0)),
            _const_spec((1, D)),
            _const_spec((D, PROJ_W)),
        ],
        out_specs=pl.BlockSpec((1, tm, PROJ_W), lambda b, t: (b, t, 0)),
        scratch_shapes=[pltpu.VMEM((tm, D), bf16)],
        compiler_params=_params(("arbitrary", "arbitrary")),
        name="in_projection",
    )(x, mod_l, gain.reshape(1, D), w_perm)


def _shift_rows(x, prev8, s):
    xs = pltpu.roll(x, s, axis=0)
    fix = pltpu.roll(prev8, s, axis=0)
    row = lax.broadcasted_iota(jnp.int32, fix.shape, 0)
    head = jnp.where(row < s, fix, xs[:8])
    return jnp.concatenate([head, xs[8:]], axis=0)


def _causal_conv(x, prev8, w_ref, taps):
    acc = x * w_ref[taps - 1:taps, :]
    for s in range(1, taps):
        acc = acc + _shift_rows(x, prev8, s) * w_ref[taps - 1 - s:taps - s, :]
    return acc


def _block_lower_inverse(low, base, top):
    n = low.shape[0]
    row = lax.broadcasted_iota(jnp.int32, (n, n), 0)
    col = lax.broadcasted_iota(jnp.int32, (n, n), 1)
    same_block = lambda size: (row // size) == (col // size)
    lb = jnp.where(same_block(base), low, 0.0)
    x = jnp.where(row == col, 1.0, 0.0).astype(f32) - lb
    p = _bdot(lb, lb)
    for _ in range(int(math.log2(base)) - 2):
        xp = _bdot(jnp.concatenate([x, p], axis=0), p)
        x = x + xp[:n]
        p = xp[n:]
    x = x + _bdot(x, p)
    size = base
    while size < top:
        off = jnp.where(jnp.logical_and(same_block(2 * size), jnp.logical_not(same_block(size))), low, 0.0)
        x = x - _bdot(_bdot(x, off), x)
        size *= 2
    return x


def _gdn_kernel(qkv_ref, gate_ref, ba_ref, convw_ref, alog_ref, dt_ref, onorm_ref, o_ref,
                q_s, k_s, v_s, gc_s, beta_s, state_s, carry_s):
    tg = qkv_ref.shape[1]
    C = GDN_CHUNK

    @pl.when(pl.program_id(1) == 0)
    def _():
        state_s[...] = jnp.zeros_like(state_s)
        carry_s[...] = jnp.zeros_like(carry_s)

    for grp, dst in enumerate((q_s, k_s, v_s)):
        for h in range(GDN_HEADS):
            c0 = grp * GROUP_W + h * HEAD_W
            x = qkv_ref[0, :, c0:c0 + HEAD_W]
            y = _silu(_causal_conv(x, carry_s[:, c0:c0 + HEAD_W], convw_ref.at[:, c0:c0 + HEAD_W], GDN_CONV))
            if grp < 2:
                y = y * lax.rsqrt(jnp.sum(y * y, axis=-1, keepdims=True) + EPS)
            if grp == 0:
                y = y * (GDN_DK ** -0.5)
            dst[:, h * HEAD_W:(h + 1) * HEAD_W] = y
    carry_s[...] = qkv_ref[0, tg - 8:tg, :]

    ba = ba_ref[0]
    beta_s[...] = 1.0 / (1.0 + jnp.exp(-ba))
    z = ba + dt_ref[...]
    g = -jnp.exp(alog_ref[...]) * (jnp.maximum(z, 0.0) + jnp.log1p(jnp.exp(-jnp.abs(z))))
    rowc = lax.broadcasted_iota(jnp.int32, g.shape, 0) % C
    s = 1
    while s < C:
        g = g + jnp.where(rowc >= s, pltpu.roll(g, s, axis=0), 0.0)
        s *= 2
    gc_s[...] = g

    H = GDN_HEADS
    n = H * C
    ri = lax.broadcasted_iota(jnp.int32, (n, n), 0)
    ci = lax.broadcasted_iota(jnp.int32, (n, n), 1)
    same_head = (ri // C) == (ci // C)
    incl = jnp.logical_and(same_head, ri >= ci)
    strict = jnp.logical_and(same_head, ri > ci)
    onorm = onorm_ref[...]

    def chunk_body(c, carry):
        r0 = pl.multiple_of(c * C, C)
        rows = pl.ds(r0, C)
        gcc = gc_s[rows, :]
        gct = gcc.T
        betac = beta_s[rows, :]
        glast = gc_s[pl.ds(r0 + C - 1, 1), :]
        ks, kbs, qs, rhs, qgs, kdecs, gcols, grows, egls = [], [], [], [], [], [], [], [], []
        for h in range(H):
            cols = slice(h * HEAD_W, (h + 1) * HEAD_W)
            q = q_s[rows, cols]
            k = k_s[rows, cols]
            v = v_s[rows, cols]
            beta = betac[:, h:h + 1]
            gcol = gcc[:, H + h:H + h + 1]
            gl = glast[:, H + h:H + h + 1]
            eg = jnp.exp(gcol)
            kb = k * beta
            ks.append(k)
            kbs.append(kb)
            qs.append(q)
            rhs.append(jnp.concatenate([v * beta, kb * eg], axis=1))
            qgs.append(q * eg)
            kdecs.append(k * jnp.exp(gl - gcol))
            gcols.append(gcol)
            grows.append(gct[H + h:H + h + 1, :])
            egls.append(jnp.exp(gl))
        gcol_all = jnp.concatenate(gcols, axis=0)
        grow_all = jnp.concatenate(grows, axis=1)
        decay = jnp.exp(jnp.where(incl, gcol_all - grow_all, NEG_BIG))
        kq = _bdot_nt(jnp.concatenate(kbs + qs, axis=0), jnp.concatenate(ks, axis=0))
        low = jnp.where(strict, kq[:n] * decay, 0.0)
        a_intra = kq[n:] * decay
        t_inv = _block_lower_inverse(low, 16, C)
        uw = _bdot(t_inv, jnp.concatenate(rhs, axis=0))
        v_news, q_states = [], []
        for h in range(H):
            hr = slice(h * C, (h + 1) * C)
            state = state_s[h]
            ws = _bdot(jnp.concatenate([uw[hr, HEAD_W:], qgs[h]], axis=0), state)
            v_new = uw[hr, :HEAD_W] - ws[:C]
            v_news.append(v_new)
            q_states.append(ws[C:])
            state_s[h] = state * egls[h] + _bdot(kdecs[h].T, v_new)
        o_all = jnp.concatenate(q_states, axis=0) + _bdot(a_intra, jnp.concatenate(v_news, axis=0))
        for h in range(H):
            cols = slice(h * HEAD_W, (h + 1) * HEAD_W)
            o = o_all[h * C:(h + 1) * C]
            on = o * lax.rsqrt(jnp.mean(o * o, axis=-1, keepdims=True) + EPS) * onorm
            o_ref[0, rows, cols] = on * _silu(gate_ref[0, rows, cols])
        return carry

    lax.fori_loop(0, tg // C, chunk_body, 0)


def _gdn_mixer(proj, conv_w, a_log, dt_bias, out_norm):
    B, T, _ = proj.shape
    tg = TOKEN_TILE
    convw = jnp.zeros((8, 3 * GROUP_W), f32).at[:GDN_CONV].set(conv_w.astype(f32))
    lane_pad = lambda v: jnp.zeros((1, HEAD_W), f32).at[0, GDN_HEADS:2 * GDN_HEADS].set(v.astype(f32))
    return pl.pallas_call(
        _gdn_kernel,
        out_shape=jax.ShapeDtypeStruct((B, T, GROUP_W), f32),
        grid=(B, T // tg),
        in_specs=[
            pl.BlockSpec((1, tg, 3 * GROUP_W), lambda b, t: (b, t, 0)),
            pl.BlockSpec((1, tg, GROUP_W), lambda b, t: (b, t, COL_GATE // GROUP_W)),
            pl.BlockSpec((1, tg, HEAD_W), lambda b, t: (b, t, COL_BA // HEAD_W)),
            _const_spec((8, 3 * GROUP_W)),
            _const_spec((1, HEAD_W)),
            _const_spec((1, HEAD_W)),
            _const_spec((1, HEAD_W)),
        ],
        out_specs=pl.BlockSpec((1, tg, GROUP_W), lambda b, t: (b, t, 0)),
        scratch_shapes=[
            pltpu.VMEM((tg, GROUP_W), f32),
            pltpu.VMEM((tg, GROUP_W), f32),
            pltpu.VMEM((tg, GROUP_W), f32),
            pltpu.VMEM((tg, HEAD_W), f32),
            pltpu.VMEM((tg, HEAD_W), f32),
            pltpu.VMEM((GDN_HEADS, GDN_DK, HEAD_W), f32),
            pltpu.VMEM((8, 3 * GROUP_W), f32),
        ],
        compiler_params=_params(("arbitrary", "arbitrary")),
        name="gdn_mixer",
    )(proj, proj, proj, convw, lane_pad(a_log), lane_pad(dt_bias), out_norm.reshape(1, HEAD_W).astype(f32))


def _half_rmsnorm(x, gain_row):
    lane = lax.broadcasted_iota(jnp.int32, x.shape, 1)
    lo = lane < DIFF_DH
    x2 = x * x
    s_lo = jnp.sum(jnp.where(lo, x2, 0.0), axis=-1, keepdims=True)
    s_hi = jnp.sum(jnp.where(lo, 0.0, x2), axis=-1, keepdims=True)
    ms = jnp.where(lo, s_lo, s_hi) * (1.0 / DIFF_DH)
    return x * lax.rsqrt(ms + EPS) * gain_row


def _attn_kernel(q_ref, k_ref, v_ref, db_ref, qn_ref, kn_ref, lam_ref, subln_ref, o_ref,
                 k_s, vt_s, q_s, m_s, acc_s, *, lambda_init):
    tq = q_ref.shape[1]
    tk = db_ref.shape[2]
    T = k_ref.shape[1]
    dv = v_ref.shape[2]
    qi = pl.program_id(2)

    @pl.when(qi == 0)
    def _():
        ones_rows = jnp.where(lax.broadcasted_iota(jnp.int32, (vt_s.shape[1] - dv, tk), 0) == 0, 1.0, 0.0)

        def prep(i, carry):
            rows = pl.ds(pl.multiple_of(i * tk, tk), tk)
            k_s[i] = _half_rmsnorm(k_ref[0, rows, :], kn_ref[...]).astype(bf16)
            vt_s[i] = jnp.concatenate([v_ref[0, rows, :].T, ones_rows], axis=0).astype(bf16)
            return carry
        lax.fori_loop(0, T // tk, prep, 0)

    qt = (_half_rmsnorm(q_ref[0], qn_ref[...]) * (DIFF_DH ** -0.5 * LOG2E)).T
    dim = lax.broadcasted_iota(jnp.int32, qt.shape, 0)
    q_s[0] = jnp.where(dim < DIFF_DH, qt, 0.0).astype(bf16)
    q_s[1] = jnp.where(dim < DIFF_DH, 0.0, qt).astype(bf16)

    m_s[...] = jnp.full_like(m_s, NEG_BIG)
    acc_s[...] = jnp.zeros_like(acc_s)

    def tiles(js, biases):
        scores = []
        for c in range(2):
            qc = q_s[c]
            ss = []
            for j, bias in zip(js, biases):
                s = jnp.dot(k_s[j], qc, preferred_element_type=f32)
                ss.append(s if bias is None else s + bias[:, c * tq:(c + 1) * tq])
            scores.append(ss)
        for c, ss in enumerate(scores):
            m_old = m_s[c]
            m_new = m_old
            for s in ss:
                m_new = jnp.maximum(m_new, jnp.max(s, axis=0, keepdims=True))
            acc = jnp.exp2(m_old - m_new) * acc_s[c]
            for j, s in zip(js, ss):
                acc = acc + jnp.dot(vt_s[j], jnp.exp2(s - m_new).astype(bf16), preferred_element_type=f32)
            acc_s[c] = acc
            m_s[c] = m_new

    @pl.when(qi == 0)
    def _():
        tiles([0], [db_ref[0, 0]])

    n_far = jnp.maximum(qi - 1, 0)
    n_grouped = (n_far // ATTN_GROUP) * ATTN_GROUP
    for rem in range(ATTN_GROUP):
        @pl.when(jnp.logical_and(qi >= 1, n_far - n_grouped == rem))
        def _():
            tiles([qi, qi - 1] + [n_grouped + t for t in range(rem)],
                  [db_ref[0, 0], db_ref[1, 0]] + [None] * rem)

    def tiles_stale_shift(js):
        m_old = [m_s[c] for c in range(2)]
        new_m = list(m_old)
        new_acc = [acc_s[c] for c in range(2)]
        units = [(j, c) for j in js for c in range(2)]
        scores = {}
        for i in range(len(units) + ATTN_LOOKAHEAD):
            if i < len(units):
                j, c = units[i]
                scores[i] = jnp.dot(k_s[j], q_s[c], preferred_element_type=f32)
            if i >= ATTN_LOOKAHEAD:
                j, c = units[i - ATTN_LOOKAHEAD]
                s = scores.pop(i - ATTN_LOOKAHEAD)
                new_m[c] = jnp.maximum(new_m[c], jnp.max(s, axis=0, keepdims=True))
                new_acc[c] = new_acc[c] + jnp.dot(vt_s[j], jnp.exp2(s - m_old[c]).astype(bf16),
                                                  preferred_element_type=f32)
        new_acc = [new_acc[c] * jnp.exp2(m_old[c] - new_m[c]) for c in range(2)]
        excess = [jnp.max(new_m[c] - m_old[c]) for c in range(2)]
        safe = jnp.maximum(excess[0], excess[1]) <= STALE_SHIFT_MARGIN

        @pl.when(safe)
        def _():
            for c in range(2):
                acc_s[c] = new_acc[c]
                m_s[c] = new_m[c]

        @pl.when(jnp.logical_not(safe))
        def _():
            tiles(js, [None] * len(js))

    def far_group(i, carry):
        tiles_stale_shift([ATTN_GROUP * i + t for t in range(ATTN_GROUP)])
        return carry
    lax.fori_loop(0, n_far // ATTN_GROUP, far_group, 0)

    lam_rows = lam_ref[...]
    lam = (jnp.exp(jnp.sum(lam_rows[0:1] * lam_rows[1:2], axis=-1, keepdims=True))
           - jnp.exp(jnp.sum(lam_rows[2:3] * lam_rows[3:4], axis=-1, keepdims=True)) + lambda_init)
    acc1 = acc_s[0]
    acc2 = acc_s[1]
    ot = acc1[:dv] * (1.0 / acc1[dv:dv + 1]) - lam * (acc2[:dv] * (1.0 / acc2[dv:dv + 1]))
    ont = ot * lax.rsqrt(jnp.mean(ot * ot, axis=0, keepdims=True) + EPS)
    o_ref[0] = ont.T * (subln_ref[...] * (1.0 - lambda_init))


def _diff_attention(proj, db, q_norm, k_norm, lq1, lk1, lq2, lk2, subln, lambda_init):
    B, T, _ = proj.shape
    tq, tk = ATTN_TQ, ATTN_TK
    two = lambda v: jnp.concatenate([v, v]).reshape(1, HEAD_W).astype(f32)
    lam_rows = jnp.zeros((8, DIFF_DH), f32).at[:4].set(jnp.stack([lq1, lk1, lq2, lk2]).astype(f32))
    qb, kb, vb = COL_DQ // HEAD_W, COL_DK // HEAD_W, COL_DV // HEAD_W
    return pl.pallas_call(
        functools.partial(_attn_kernel, lambda_init=lambda_init),
        out_shape=jax.ShapeDtypeStruct((B, T, GROUP_W), f32),
        grid=(B, DIFF_HEADS, T // tq),
        in_specs=[
            pl.BlockSpec((1, tq, HEAD_W), lambda b, h, i: (b, i, qb + h)),
            pl.BlockSpec((1, T, HEAD_W), lambda b, h, i: (b, 0, kb + h)),
            pl.BlockSpec((1, T, HEAD_W), lambda b, h, i: (b, 0, vb + h)),
            pl.BlockSpec((2, 1, tk, 2 * tq), lambda b, h, i: (0, h, 0, 0)),
            _const_spec((1, HEAD_W)),
            _const_spec((1, HEAD_W)),
            _const_spec((8, DIFF_DH)),
            _const_spec((1, HEAD_W)),
        ],
        out_specs=pl.BlockSpec((1, tq, HEAD_W), lambda b, h, i: (b, i, h)),
        scratch_shapes=[
            pltpu.VMEM((T // tk, tk, HEAD_W), bf16),
            pltpu.VMEM((T // tk, HEAD_W + BF16_ROWS, tk), bf16),
            pltpu.VMEM((2, HEAD_W, tq), bf16),
            pltpu.VMEM((2, 1, tq), f32),
            pltpu.VMEM((2, HEAD_W + BF16_ROWS, tq), f32),
        ],
        compiler_params=_params(("arbitrary", "arbitrary", "arbitrary")),
        name="diff_attention",
    )(proj, proj, proj, db, two(q_norm), two(k_norm), lam_rows, subln.reshape(1, HEAD_W).astype(f32))


def _outffn_kernel(x_ref, yg_ref, yd_ref, mod_ref, g_ref, wout_ref, wup_ref, cw_ref, wdown_ref, o_ref,
                   h_s, acc_s, carry_s):
    nchunks = wup_ref.shape[0]
    fc = wdown_ref.shape[1]

    @pl.when(pl.program_id(1) == 0)
    def _():
        carry_s[...] = jnp.zeros_like(carry_s)

    half = yg_ref.shape[2]
    y = (jnp.dot(yg_ref[0].astype(bf16), wout_ref[:half, :], preferred_element_type=f32)
         + jnp.dot(yd_ref[0].astype(bf16), wout_ref[half:, :], preferred_element_type=f32))
    x1 = x_ref[0] + mod_ref[0, 2:3, :] * y
    o_ref[0] = x1
    h_s[...] = _modulated_norm(x1, g_ref[...], mod_ref, 3, 4).astype(bf16)
    acc_s[...] = jnp.zeros_like(acc_s)
    tm = h_s.shape[0]

    tr = tm // FFN_ROW_SPLIT

    def up(j):
        return [jnp.dot(h_s[r0:r0 + tr, :], wup_ref[j], preferred_element_type=f32)
                for r0 in range(0, tm, tr)]

    us = up(0)
    for j in range(nchunks):
        us_next = up(j + 1) if j + 1 < nchunks else None
        cw = cw_ref.at[j]
        prev8 = carry_s[j]
        for i, u in enumerate(us):
            yc = _causal_conv(u, prev8, cw, FFN_CONV) + cw[FFN_CONV:FFN_CONV + 1, :]
            prev8 = u[tr - 8:, :]
            act = _silu(yc[:, fc:]) * yc[:, :fc]
            acc_s[i * tr:(i + 1) * tr, :] += jnp.dot(act.astype(bf16), wdown_ref[j], preferred_element_type=f32)
        carry_s[j] = prev8
        us = us_next
    o_ref[0] = o_ref[0] + mod_ref[0, 5:6, :] * acc_s[...]


def _out_ffn(x, y_gdn, y_diff, mod_l, gain, w_out, w_up_c, conv_c, w_down_c):
    B, T, D = x.shape
    tm = TOKEN_TILE
    nchunks, _, fc2 = w_up_c.shape
    fc = fc2 // 2
    return pl.pallas_call(
        _outffn_kernel,
        out_shape=jax.ShapeDtypeStruct((B, T, D), f32),
        grid=(B, T // tm),
        in_specs=[
            pl.BlockSpec((1, tm, D), lambda b, t: (b, t, 0)),
            pl.BlockSpec((1, tm, GROUP_W), lambda b, t: (b, t, 0)),
            pl.BlockSpec((1, tm, GROUP_W), lambda b, t: (b, t, 0)),
            pl.BlockSpec((1, 6, D), lambda b, t: (b, 0, 0)),
            _const_spec((1, D)),
            _const_spec((2 * GROUP_W, D)),
            _const_spec((nchunks, D, fc2)),
            _const_spec((nchunks, 8, fc2)),
            _const_spec((nchunks, fc, D)),
        ],
        out_specs=pl.BlockSpec((1, tm, D), lambda b, t: (b, t, 0)),
        scratch_shapes=[
            pltpu.VMEM((tm, D), bf16),
            pltpu.VMEM((tm, D), f32),
            pltpu.VMEM((nchunks, 8, fc2), f32),
        ],
        compiler_params=_params(("arbitrary", "arbitrary")),
        name="out_ffn",
    )(x, y_gdn, y_diff, mod_l, gain.reshape(1, D), w_out, w_up_c, conv_c, w_down_c)


def _regroup_w_in(w_in_l):
    D = w_in_l.shape[0]
    o_b = 4 * GROUP_W
    o_d = o_b + 2 * GDN_HEADS
    small = jnp.zeros((D, HEAD_W), w_in_l.dtype).at[:, :2 * GDN_HEADS].set(w_in_l[:, o_b:o_d])
    return jnp.concatenate([w_in_l[:, :o_b], w_in_l[:, o_d:], small], axis=1).astype(bf16)


def _chunk_ffn(ffn_up_l, conv_w_l, conv_b_l, ffn_down_l, fc):
    D, two_f = ffn_up_l.shape
    F = two_f // 2
    n = F // fc
    pair = lambda m: jnp.concatenate([m[..., :F].reshape(m.shape[:-1] + (n, fc)),
                                      m[..., F:].reshape(m.shape[:-1] + (n, fc))], axis=-1)
    w_up_c = jnp.moveaxis(pair(ffn_up_l), -2, 0).astype(bf16)
    conv_rows = jnp.concatenate([conv_w_l.astype(f32), conv_b_l.astype(f32)[None]], axis=0)
    conv_c = jnp.moveaxis(pair(conv_rows), -2, 0)
    conv_c = jnp.zeros((n, 8, 2 * fc), f32).at[:, :FFN_CONV + 1].set(conv_c)
    w_down_c = ffn_down_l.reshape(n, fc, ffn_down_l.shape[1]).astype(bf16)
    return w_up_c, conv_c, w_down_c


def kernel(x, c, w_ada, b_ada, norm_mix, norm_ffn, w_in, gdn_conv_w, gdn_a_log, gdn_dt_bias, gdn_out_norm,
           diff_q_norm, diff_k_norm, diff_lambda_q1, diff_lambda_k1, diff_lambda_q2, diff_lambda_k2, diff_subln,
           rel_bias, w_out, ffn_up, ffn_conv_w, ffn_conv_b, ffn_down):
    B, T, D = x.shape
    depth = w_in.shape[0]
    mod = _ada_modulation(c, w_ada, b_ada).reshape(depth, B, 6, D)
    db = _bias_tiles(rel_bias, ATTN_TQ, ATTN_TK)
    for l in range(depth):
        proj = _in_projection(x, mod[l], norm_mix[l], _regroup_w_in(w_in[l]))
        y_gdn = _gdn_mixer(proj, gdn_conv_w[l], gdn_a_log[l], gdn_dt_bias[l], gdn_out_norm[l])
        lambda_init = 0.8 - 0.6 * math.exp(-0.3 * l)
        y_diff = _diff_attention(proj, db, diff_q_norm[l], diff_k_norm[l], diff_lambda_q1[l], diff_lambda_k1[l],
                                 diff_lambda_q2[l], diff_lambda_k2[l], diff_subln[l], lambda_init)
        w_up_c, conv_c, w_down_c = _chunk_ffn(ffn_up[l], ffn_conv_w[l], ffn_conv_b[l], ffn_down[l], FFN_CHUNK)
        x = _out_ffn(x, y_gdn, y_diff, mod[l], norm_ffn[l], w_out[l].astype(bf16), w_up_c, conv_c, w_down_c)
    return x
```

```python
import functools
import math

import numpy as np
import jax
import jax.numpy as jnp
from jax import lax
from jax.experimental import pallas as pl
from jax.experimental.pallas import tpu as pltpu

f32 = jnp.float32
bf16 = jnp.bfloat16

EPS = 1e-6
GDN_HEADS = 4
GDN_DK = 128
GDN_CONV = 4
GDN_CHUNK = 64
GDN_LOCKSTEP = 4
DIFF_HEADS = 4
DIFF_DH = 64
REL_BUCKETS = 32
REL_MAX_DIST = 128
FFN_CONV = 3
HEAD_W = 128
GROUP_W = GDN_HEADS * HEAD_W
NEG_BIG = -1e30
LOG2E = math.log2(math.e)
BF16_ROWS = 16

VMEM_LIMIT_BYTES = 56 * 1024 * 1024

COL_GQ, COL_GK, COL_GV, COL_GATE, COL_DQ, COL_DK, COL_DV, COL_BA = (
    0, GROUP_W, 2 * GROUP_W, 3 * GROUP_W, 4 * GROUP_W, 5 * GROUP_W, 6 * GROUP_W, 7 * GROUP_W)
PROJ_W = 7 * GROUP_W + HEAD_W

TOKEN_TILE = 512
ATTN_TQ = 256
ATTN_TK = 256
ATTN_GROUP = 4
ATTN_LOOKAHEAD = 4
STALE_SHIFT_MARGIN = 64.0
FFN_CHUNK = 256
FFN_ROW_SPLIT = 2


def _bdot(a, b):
    return jnp.dot(a.astype(bf16), b.astype(bf16), preferred_element_type=f32)


def _bdot_nt(a, b):
    return lax.dot_general(a.astype(bf16), b.astype(bf16), (((1,), (1,)), ((), ())),
                           preferred_element_type=f32)


def _silu(x):
    return x * (1.0 / (1.0 + jnp.exp(-x)))


def _params(semantics):
    return pltpu.CompilerParams(dimension_semantics=semantics, vmem_limit_bytes=VMEM_LIMIT_BYTES)


def _const_spec(shape):
    nd = len(shape)
    return pl.BlockSpec(shape, lambda *_: (0,) * nd, pipeline_mode=pl.Buffered(1))


def _ada_kernel(c_ref, w_ref, b_ref, o_ref):
    ca = _silu(c_ref[...])
    o_ref[0] = _bdot(ca, w_ref[0]) + b_ref[0]


def _ada_modulation(c, w_ada, b_ada):
    L, D, N = w_ada.shape
    B = c.shape[0]
    tn = 1536
    return pl.pallas_call(
        _ada_kernel,
        out_shape=jax.ShapeDtypeStruct((L, B, N), f32),
        grid=(L, N // tn),
        in_specs=[
            pl.BlockSpec((B, D), lambda l, n: (0, 0)),
            pl.BlockSpec((1, D, tn), lambda l, n: (l, 0, n)),
            pl.BlockSpec((1, 1, tn), lambda l, n: (l, 0, n)),
        ],
        out_specs=pl.BlockSpec((1, B, tn), lambda l, n: (l, 0, n)),
        compiler_params=_params(("arbitrary", "arbitrary")),
        name="ada_modulation",
    )(c, w_ada, b_ada.reshape(L, 1, N))


def _bucket_of_distance(n):
    max_exact = REL_BUCKETS // 2
    nf = np.maximum(n, 1).astype(np.float32)
    large = max_exact + (np.log(nf / np.float32(max_exact)) / np.float32(math.log(REL_MAX_DIST / max_exact))
                         * np.float32(REL_BUCKETS - max_exact)).astype(np.int32)
    large = np.minimum(large, REL_BUCKETS - 1)
    return np.where(n < max_exact, n, large).astype(np.int32)


def _bias_bucket_tiles(tq, tk):
    assert tq == tk
    i = np.arange(tq)[None, :]
    j = np.arange(tk)[:, None]
    d0 = i - j
    diag = np.where(d0 >= 0, _bucket_of_distance(np.maximum(d0, 0)), -1)
    sub = _bucket_of_distance(tk + d0)
    assert _bucket_of_distance(np.array([tk + 1]))[0] == REL_BUCKETS - 1
    both = np.stack([diag, sub]).astype(np.int32)
    return np.concatenate([both, both], axis=2)


def _bias_tile_kernel(rel_ref, bk_ref, o_ref):
    h = pl.program_id(1)
    bk = bk_ref[0]
    last = rel_ref[REL_BUCKETS - 1, h]
    acc = jnp.where(bk < 0, NEG_BIG, 0.0).astype(f32)
    for b in range(REL_BUCKETS - 1):
        acc = jnp.where(bk == b, (rel_ref[b, h] - last) * LOG2E, acc)
    o_ref[0, 0] = acc


def _bias_tiles(rel_bias, tq, tk):
    buckets = jnp.asarray(_bias_bucket_tiles(tq, tk))
    return pl.pallas_call(
        _bias_tile_kernel,
        out_shape=jax.ShapeDtypeStruct((2, DIFF_HEADS, tk, 2 * tq), f32),
        grid=(2, DIFF_HEADS),
        in_specs=[
            pl.BlockSpec(memory_space=pltpu.SMEM),
            pl.BlockSpec((1, tk, 2 * tq), lambda k, h: (k, 0, 0)),
        ],
        out_specs=pl.BlockSpec((1, 1, tk, 2 * tq), lambda k, h: (k, h, 0, 0)),
        compiler_params=_params(("arbitrary", "arbitrary")),
        name="rel_bias_tiles",
    )(rel_bias.astype(f32), buckets)


def _modulated_norm(x, gain_row, mod_ref, shift_idx, scale_idx):
    ms = jnp.mean(x * x, axis=-1, keepdims=True)
    y = x * lax.rsqrt(ms + EPS)
    return y * (gain_row * (1.0 + mod_ref[0, scale_idx:scale_idx + 1, :])) + mod_ref[0, shift_idx:shift_idx + 1, :]


def _inproj_kernel(x_ref, mod_ref, g_ref, w_ref, o_ref, h_scr):
    h_scr[...] = _modulated_norm(x_ref[0], g_ref[...], mod_ref, 0, 1).astype(bf16)
    n = o_ref.shape[2]
    for c0 in range(0, n, GROUP_W):
        c1 = min(c0 + GROUP_W, n)
        o_ref[0, :, c0:c1] = jnp.dot(h_scr[...], w_ref[:, c0:c1], preferred_element_type=f32)


def _in_projection(x, mod_l, gain, w_perm):
    B, T, D = x.shape
    tm = TOKEN_TILE
    return pl.pallas_call(
        _inproj_kernel,
        out_shape=jax.ShapeDtypeStruct((B, T, PROJ_W), f32),
        grid=(B, T // tm),
        in_specs=[
            pl.BlockSpec((1, tm, D), lambda b, t: (b, t, 0)),
            pl.BlockSpec((1, 6, D), lambda b, t: (b, 0, 0)),
            _const_spec((1, D)),
            _const_spec((D, PROJ_W)),
        ],
        out_specs=pl.BlockSpec((1, tm, PROJ_W), lambda b, t: (b, t, 0)),
        scratch_shapes=[pltpu.VMEM((tm, D), bf16)],
        compiler_params=_params(("arbitrary", "arbitrary")),
        name="in_projection",
    )(x, mod_l, gain.reshape(1, D), w_perm)


def _shift_rows(x, prev8, s):
    xs = pltpu.roll(x, s, axis=0)
    fix = pltpu.roll(prev8, s, axis=0)
    row = lax.broadcasted_iota(jnp.int32, fix.shape, 0)
    head = jnp.where(row < s, fix, xs[:8])
    return jnp.concatenate([head, xs[8:]], axis=0)


def _causal_conv(x, prev8, w_ref, taps):
    acc = x * w_ref[taps - 1:taps, :]
    for s in range(1, taps):
        acc = acc + _shift_rows(x, prev8, s) * w_ref[taps - 1 - s:taps - s, :]
    return acc


def _block_lower_inverses(lows, base, top):
    n = lows[0].shape[0]
    row = lax.broadcasted_iota(jnp.int32, (n, n), 0)
    col = lax.broadcasted_iota(jnp.int32, (n, n), 1)
    same_block = lambda size: (row // size) == (col // size)
    eye = jnp.where(row == col, 1.0, 0.0).astype(f32)
    base_mask = same_block(base)
    lbs = [jnp.where(base_mask, low, 0.0) for low in lows]
    xs = [eye - lb for lb in lbs]
    lbs = [lb.astype(bf16) for lb in lbs]
    ps = [jnp.dot(lb, lb, preferred_element_type=f32) for lb in lbs]
    for _ in range(int(math.log2(base)) - 2):
        pbs = [p.astype(bf16) for p in ps]
        xps = [jnp.dot(jnp.concatenate([x.astype(bf16), pb], axis=0), pb, preferred_element_type=f32)
               for x, pb in zip(xs, pbs)]
        xs = [x + xp[:n] for x, xp in zip(xs, xps)]
        ps = [xp[n:] for xp in xps]
    xs = [x + _bdot(x, p) for x, p in zip(xs, ps)]
    size = base
    while size < top:
        off_mask = jnp.logical_and(same_block(2 * size), jnp.logical_not(same_block(size)))
        offs = [jnp.where(off_mask, low, 0.0).astype(bf16) for low in lows]
        xbs = [x.astype(bf16) for x in xs]
        ts = [jnp.dot(xb, off, preferred_element_type=f32) for xb, off in zip(xbs, offs)]
        ts = [jnp.dot(t.astype(bf16), xb, preferred_element_type=f32) for t, xb in zip(ts, xbs)]
        xs = [x - t for x, t in zip(xs, ts)]
        size *= 2
    return xs


def _gdn_kernel(qkv_ref, gate_ref, ba_ref, convw_ref, alog_ref, dt_ref, onorm_ref, o_ref,
                q_s, k_s, v_s, gc_s, beta_s, state_s, carry_s):
    tg = qkv_ref.shape[1]
    C = GDN_CHUNK

    @pl.when(pl.program_id(1) == 0)
    def _():
        state_s[...] = jnp.zeros_like(state_s)
        carry_s[...] = jnp.zeros_like(carry_s)

    for grp, dst in enumerate((q_s, k_s, v_s)):
        for h in range(GDN_HEADS):
            c0 = grp * GROUP_W + h * HEAD_W
            x = qkv_ref[0, :, c0:c0 + HEAD_W]
            y = _silu(_causal_conv(x, carry_s[:, c0:c0 + HEAD_W], convw_ref.at[:, c0:c0 + HEAD_W], GDN_CONV))
            if grp < 2:
                y = y * lax.rsqrt(jnp.sum(y * y, axis=-1, keepdims=True) + EPS)
            if grp == 0:
                y = y * (GDN_DK ** -0.5)
            dst[:, h * HEAD_W:(h + 1) * HEAD_W] = y
    carry_s[...] = qkv_ref[0, tg - 8:tg, :]

    ba = ba_ref[0]
    beta_s[...] = 1.0 / (1.0 + jnp.exp(-ba))
    z = ba + dt_ref[...]
    g = -jnp.exp(alog_ref[...]) * (jnp.maximum(z, 0.0) + jnp.log1p(jnp.exp(-jnp.abs(z))))
    rowc = lax.broadcasted_iota(jnp.int32, g.shape, 0) % C
    s = 1
    while s < C:
        g = g + jnp.where(rowc >= s, pltpu.roll(g, s, axis=0), 0.0)
        s *= 2
    gc_s[...] = g

    H = GDN_HEADS
    n = H * C
    ri = lax.broadcasted_iota(jnp.int32, (n, n), 0)
    ci = lax.broadcasted_iota(jnp.int32, (n, n), 1)
    same_head = (ri // C) == (ci // C)
    incl = jnp.logical_and(same_head, ri >= ci)
    strict = jnp.logical_and(same_head, ri > ci)
    onorm = onorm_ref[...]

    def chunk_operands(c):
        rows = slice(c * C, (c + 1) * C)
        gcc = gc_s[rows, :]
        gct = gcc.T
        betac = beta_s[rows, :]
        glast = gc_s[(c + 1) * C - 1:(c + 1) * C, :]
        ks, kbs, qs, rhs, qgs, kdts, gcols, grows, egls = [], [], [], [], [], [], [], [], []
        for h in range(H):
            cols = slice(h * HEAD_W, (h + 1) * HEAD_W)
            q = q_s[rows, cols]
            k = k_s[rows, cols]
            v = v_s[rows, cols]
            beta = betac[:, h:h + 1]
            gcol = gcc[:, H + h:H + h + 1]
            gl = glast[:, H + h:H + h + 1]
            eg = jnp.exp(gcol)
            kb = k * beta
            ks.append(k)
            kbs.append(kb)
            qs.append(q)
            rhs.append(jnp.concatenate([v * beta, kb * eg], axis=1))
            qgs.append((q * eg).astype(bf16))
            kdts.append((k * jnp.exp(gl - gcol)).T.astype(bf16))
            gcols.append(gcol)
            grows.append(gct[H + h:H + h + 1, :])
            egls.append(jnp.exp(gl))
        gcol_all = jnp.concatenate(gcols, axis=0)
        grow_all = jnp.concatenate(grows, axis=1)
        decay = jnp.exp(jnp.where(incl, gcol_all - grow_all, NEG_BIG))
        return dict(kbq=jnp.concatenate(kbs + qs, axis=0).astype(bf16), k_all=jnp.concatenate(ks, axis=0).astype(bf16),
                    rhs=jnp.concatenate(rhs, axis=0).astype(bf16), decay=decay, qg=qgs, kdt=kdts, egl=egls)

    chunks = []
    for c0 in range(0, tg // C, GDN_LOCKSTEP):
        ops = [chunk_operands(c) for c in range(c0, c0 + GDN_LOCKSTEP)]
        kqs = [lax.dot_general(o["kbq"], o["k_all"], (((1,), (1,)), ((), ())), preferred_element_type=f32)
               for o in ops]
        lows = [jnp.where(strict, kq[:n] * o["decay"], 0.0) for kq, o in zip(kqs, ops)]
        for kq, o in zip(kqs, ops):
            o["a"] = (kq[n:] * o["decay"]).astype(bf16)
        t_invs = _block_lower_inverses(lows, 16, C)
        for t_inv, o in zip(t_invs, ops):
            o["uw"] = jnp.dot(t_inv.astype(bf16), o["rhs"], preferred_element_type=f32)
        chunks.extend(ops)

    states = [state_s[h] for h in range(H)]
    for c, o in enumerate(chunks):
        rows = slice(c * C, (c + 1) * C)
        uw = o["uw"]
        wss = [jnp.dot(jnp.concatenate([uw[h * C:(h + 1) * C, HEAD_W:].astype(bf16), o["qg"][h]], axis=0),
                       states[h].astype(bf16), preferred_element_type=f32) for h in range(H)]
        v_news = [uw[h * C:(h + 1) * C, :HEAD_W] - wss[h][:C] for h in range(H)]
        states = [states[h] * o["egl"][h]
                  + jnp.dot(o["kdt"][h], v_news[h].astype(bf16), preferred_element_type=f32) for h in range(H)]
        o_all = (jnp.concatenate([ws[C:] for ws in wss], axis=0)
                 + jnp.dot(o["a"], jnp.concatenate(v_news, axis=0).astype(bf16), preferred_element_type=f32))
        for h in range(H):
            cols = slice(h * HEAD_W, (h + 1) * HEAD_W)
            oh = o_all[h * C:(h + 1) * C]
            on = oh * lax.rsqrt(jnp.mean(oh * oh, axis=-1, keepdims=True) + EPS) * onorm
            o_ref[0, rows, cols] = on * _silu(gate_ref[0, rows, cols])
    for h in range(H):
        state_s[h] = states[h]


def _gdn_mixer(proj, conv_w, a_log, dt_bias, out_norm):
    B, T, _ = proj.shape
    tg = TOKEN_TILE
    convw = jnp.zeros((8, 3 * GROUP_W), f32).at[:GDN_CONV].set(conv_w.astype(f32))
    lane_pad = lambda v: jnp.zeros((1, HEAD_W), f32).at[0, GDN_HEADS:2 * GDN_HEADS].set(v.astype(f32))
    return pl.pallas_call(
        _gdn_kernel,
        out_shape=jax.ShapeDtypeStruct((B, T, GROUP_W), f32),
        grid=(B, T // tg),
        in_specs=[
            pl.BlockSpec((1, tg, 3 * GROUP_W), lambda b, t: (b, t, 0)),
            pl.BlockSpec((1, tg, GROUP_W), lambda b, t: (b, t, COL_GATE // GROUP_W)),
            pl.BlockSpec((1, tg, HEAD_W), lambda b, t: (b, t, COL_BA // HEAD_W)),
            _const_spec((8, 3 * GROUP_W)),
            _const_spec((1, HEAD_W)),
            _const_spec((1, HEAD_W)),
            _const_spec((1, HEAD_W)),
        ],
        out_specs=pl.BlockSpec((1, tg, GROUP_W), lambda b, t: (b, t, 0)),
        scratch_shapes=[
            pltpu.VMEM((tg, GROUP_W), f32),
            pltpu.VMEM((tg, GROUP_W), f32),
            pltpu.VMEM((tg, GROUP_W), f32),
            pltpu.VMEM((tg, HEAD_W), f32),
            pltpu.VMEM((tg, HEAD_W), f32),
            pltpu.VMEM((GDN_HEADS, GDN_DK, HEAD_W), f32),
            pltpu.VMEM((8, 3 * GROUP_W), f32),
        ],
        compiler_params=_params(("arbitrary", "arbitrary")),
        name="gdn_mixer",
    )(proj, proj, proj, convw, lane_pad(a_log), lane_pad(dt_bias), out_norm.reshape(1, HEAD_W).astype(f32))


def _half_rmsnorm(x, gain_row):
    lane = lax.broadcasted_iota(jnp.int32, x.shape, 1)
    lo = lane < DIFF_DH
    x2 = x * x
    s_lo = jnp.sum(jnp.where(lo, x2, 0.0), axis=-1, keepdims=True)
    s_hi = jnp.sum(jnp.where(lo, 0.0, x2), axis=-1, keepdims=True)
    ms = jnp.where(lo, s_lo, s_hi) * (1.0 / DIFF_DH)
    return x * lax.rsqrt(ms + EPS) * gain_row


def _attn_kernel(q_ref, k_ref, v_ref, db_ref, qn_ref, kn_ref, lam_ref, subln_ref, o_ref,
                 k_s, vt_s, q_s, m_s, acc_s, *, lambda_init):
    tq = q_ref.shape[1]
    tk = db_ref.shape[2]
    T = k_ref.shape[1]
    dv = v_ref.shape[2]
    qi = pl.program_id(2)

    @pl.when(qi == 0)
    def _():
        ones_rows = jnp.where(lax.broadcasted_iota(jnp.int32, (vt_s.shape[1] - dv, tk), 0) == 0, 1.0, 0.0)

        def prep(i, carry):
            rows = pl.ds(pl.multiple_of(i * tk, tk), tk)
            k_s[i] = _half_rmsnorm(k_ref[0, rows, :], kn_ref[...]).astype(bf16)
            vt_s[i] = jnp.concatenate([v_ref[0, rows, :].T, ones_rows], axis=0).astype(bf16)
            return carry
        lax.fori_loop(0, T // tk, prep, 0)

    qt = (_half_rmsnorm(q_ref[0], qn_ref[...]) * (DIFF_DH ** -0.5 * LOG2E)).T
    dim = lax.broadcasted_iota(jnp.int32, qt.shape, 0)
    q_s[0] = jnp.where(dim < DIFF_DH, qt, 0.0).astype(bf16)
    q_s[1] = jnp.where(dim < DIFF_DH, 0.0, qt).astype(bf16)

    m_s[...] = jnp.full_like(m_s, NEG_BIG)
    acc_s[...] = jnp.zeros_like(acc_s)

    def tiles(js, biases):
        scores = []
        for c in range(2):
            qc = q_s[c]
            ss = []
            for j, bias in zip(js, biases):
                s = jnp.dot(k_s[j], qc, preferred_element_type=f32)
                ss.append(s if bias is None else s + bias[:, c * tq:(c + 1) * tq])
            scores.append(ss)
        for c, ss in enumerate(scores):
            m_old = m_s[c]
            m_new = m_old
            for s in ss:
                m_new = jnp.maximum(m_new, jnp.max(s, axis=0, keepdims=True))
            acc = jnp.exp2(m_old - m_new) * acc_s[c]
            for j, s in zip(js, ss):
                acc = acc + jnp.dot(vt_s[j], jnp.exp2(s - m_new).astype(bf16), preferred_element_type=f32)
            acc_s[c] = acc
            m_s[c] = m_new

    @pl.when(qi == 0)
    def _():
        tiles([0], [db_ref[0, 0]])

    n_far = jnp.maximum(qi - 1, 0)
    n_grouped = (n_far // ATTN_GROUP) * ATTN_GROUP
    for rem in range(ATTN_GROUP):
        @pl.when(jnp.logical_and(qi >= 1, n_far - n_grouped == rem))
        def _():
            tiles([qi, qi - 1] + [n_grouped + t for t in range(rem)],
                  [db_ref[0, 0], db_ref[1, 0]] + [None] * rem)

    def tiles_stale_shift(js):
        m_old = [m_s[c] for c in range(2)]
        new_m = list(m_old)
        new_acc = [acc_s[c] for c in range(2)]
        units = [(j, c) for j in js for c in range(2)]
        scores = {}
        for i in range(len(units) + ATTN_LOOKAHEAD):
            if i < len(units):
                j, c = units[i]
                scores[i] = jnp.dot(k_s[j], q_s[c], preferred_element_type=f32)
            if i >= ATTN_LOOKAHEAD:
                j, c = units[i - ATTN_LOOKAHEAD]
                s = scores.pop(i - ATTN_LOOKAHEAD)
                new_m[c] = jnp.maximum(new_m[c], jnp.max(s, axis=0, keepdims=True))
                new_acc[c] = new_acc[c] + jnp.dot(vt_s[j], jnp.exp2(s - m_old[c]).astype(bf16),
                                                  preferred_element_type=f32)
        new_acc = [new_acc[c] * jnp.exp2(m_old[c] - new_m[c]) for c in range(2)]
        excess = [jnp.max(new_m[c] - m_old[c]) for c in range(2)]
        safe = jnp.maximum(excess[0], excess[1]) <= STALE_SHIFT_MARGIN

        @pl.when(safe)
        def _():
            for c in range(2):
                acc_s[c] = new_acc[c]
                m_s[c] = new_m[c]

        @pl.when(jnp.logical_not(safe))
        def _():
            tiles(js, [None] * len(js))

    def far_group(i, carry):
        tiles_stale_shift([ATTN_GROUP * i + t for t in range(ATTN_GROUP)])
        return carry
    lax.fori_loop(0, n_far // ATTN_GROUP, far_group, 0)

    lam_rows = lam_ref[...]
    lam = (jnp.exp(jnp.sum(lam_rows[0:1] * lam_rows[1:2], axis=-1, keepdims=True))
           - jnp.exp(jnp.sum(lam_rows[2:3] * lam_rows[3:4], axis=-1, keepdims=True)) + lambda_init)
    acc1 = acc_s[0]
    acc2 = acc_s[1]
    ot = acc1[:dv] * (1.0 / acc1[dv:dv + 1]) - lam * (acc2[:dv] * (1.0 / acc2[dv:dv + 1]))
    ont = ot * lax.rsqrt(jnp.mean(ot * ot, axis=0, keepdims=True) + EPS)
    o_ref[0] = ont.T * (subln_ref[...] * (1.0 - lambda_init))


def _diff_attention(proj, db, q_norm, k_norm, lq1, lk1, lq2, lk2, subln, lambda_init):
    B, T, _ = proj.shape
    tq, tk = ATTN_TQ, ATTN_TK
    two = lambda v: jnp.concatenate([v, v]).reshape(1, HEAD_W).astype(f32)
    lam_rows = jnp.zeros((8, DIFF_DH), f32).at[:4].set(jnp.stack([lq1, lk1, lq2, lk2]).astype(f32))
    qb, kb, vb = COL_DQ // HEAD_W, COL_DK // HEAD_W, COL_DV // HEAD_W
    return pl.pallas_call(
        functools.partial(_attn_kernel, lambda_init=lambda_init),
        out_shape=jax.ShapeDtypeStruct((B, T, GROUP_W), f32),
        grid=(B, DIFF_HEADS, T // tq),
        in_specs=[
            pl.BlockSpec((1, tq, HEAD_W), lambda b, h, i: (b, i, qb + h)),
            pl.BlockSpec((1, T, HEAD_W), lambda b, h, i: (b, 0, kb + h)),
            pl.BlockSpec((1, T, HEAD_W), lambda b, h, i: (b, 0, vb + h)),
            pl.BlockSpec((2, 1, tk, 2 * tq), lambda b, h, i: (0, h, 0, 0)),
            _const_spec((1, HEAD_W)),
            _const_spec((1, HEAD_W)),
            _const_spec((8, DIFF_DH)),
            _const_spec((1, HEAD_W)),
        ],
        out_specs=pl.BlockSpec((1, tq, HEAD_W), lambda b, h, i: (b, i, h)),
        scratch_shapes=[
            pltpu.VMEM((T // tk, tk, HEAD_W), bf16),
            pltpu.VMEM((T // tk, HEAD_W + BF16_ROWS, tk), bf16),
            pltpu.VMEM((2, HEAD_W, tq), bf16),
            pltpu.VMEM((2, 1, tq), f32),
            pltpu.VMEM((2, HEAD_W + BF16_ROWS, tq), f32),
        ],
        compiler_params=_params(("arbitrary", "arbitrary", "arbitrary")),
        name="diff_attention",
    )(proj, proj, proj, db, two(q_norm), two(k_norm), lam_rows, subln.reshape(1, HEAD_W).astype(f32))


def _outffn_kernel(x_ref, yg_ref, yd_ref, mod_ref, g_ref, wout_ref, wup_ref, cw_ref, wdown_ref, o_ref,
                   h_s, acc_s, carry_s):
    nchunks = wup_ref.shape[0]
    fc = wdown_ref.shape[1]

    @pl.when(pl.program_id(1) == 0)
    def _():
        carry_s[...] = jnp.zeros_like(carry_s)

    half = yg_ref.shape[2]
    y = (jnp.dot(yg_ref[0].astype(bf16), wout_ref[:half, :], preferred_element_type=f32)
         + jnp.dot(yd_ref[0].astype(bf16), wout_ref[half:, :], preferred_element_type=f32))
    x1 = x_ref[0] + mod_ref[0, 2:3, :] * y
    o_ref[0] = x1
    h_s[...] = _modulated_norm(x1, g_ref[...], mod_ref, 3, 4).astype(bf16)
    acc_s[...] = jnp.zeros_like(acc_s)
    tm = h_s.shape[0]

    tr = tm // FFN_ROW_SPLIT

    def up(j):
        return [jnp.dot(h_s[r0:r0 + tr, :], wup_ref[j], preferred_element_type=f32)
                for r0 in range(0, tm, tr)]

    us = up(0)
    for j in range(nchunks):
        us_next = up(j + 1) if j + 1 < nchunks else None
        cw = cw_ref.at[j]
        prev8 = carry_s[j]
        for i, u in enumerate(us):
            yc = _causal_conv(u, prev8, cw, FFN_CONV) + cw[FFN_CONV:FFN_CONV + 1, :]
            prev8 = u[tr - 8:, :]
            act = _silu(yc[:, fc:]) * yc[:, :fc]
            acc_s[i * tr:(i + 1) * tr, :] += jnp.dot(act.astype(bf16), wdown_ref[j], preferred_element_type=f32)
        carry_s[j] = prev8
        us = us_next
    o_ref[0] = o_ref[0] + mod_ref[0, 5:6, :] * acc_s[...]


def _out_ffn(x, y_gdn, y_diff, mod_l, gain, w_out, w_up_c, conv_c, w_down_c):
    B, T, D = x.shape
    tm = TOKEN_TILE
    nchunks, _, fc2 = w_up_c.shape
    fc = fc2 // 2
    return pl.pallas_call(
        _outffn_kernel,
        out_shape=jax.ShapeDtypeStruct((B, T, D), f32),
        grid=(B, T // tm),
        in_specs=[
            pl.BlockSpec((1, tm, D), lambda b, t: (b, t, 0)),
            pl.BlockSpec((1, tm, GROUP_W), lambda b, t: (b, t, 0)),
            pl.BlockSpec((1, tm, GROUP_W), lambda b, t: (b, t, 0)),
            pl.BlockSpec((1, 6, D), lambda b, t: (b, 0, 0)),
            _const_spec((1, D)),
            _const_spec((2 * GROUP_W, D)),
            _const_spec((nchunks, D, fc2)),
            _const_spec((nchunks, 8, fc2)),
            _const_spec((nchunks, fc, D)),
        ],
        out_specs=pl.BlockSpec((1, tm, D), lambda b, t: (b, t, 0)),
        scratch_shapes=[
            pltpu.VMEM((tm, D), bf16),
            pltpu.VMEM((tm, D), f32),
            pltpu.VMEM((nchunks, 8, fc2), f32),
        ],
        compiler_params=_params(("arbitrary", "arbitrary")),
        name="out_ffn",
    )(x, y_gdn, y_diff, mod_l, gain.reshape(1, D), w_out, w_up_c, conv_c, w_down_c)


def _regroup_w_in(w_in_l):
    D = w_in_l.shape[0]
    o_b = 4 * GROUP_W
    o_d = o_b + 2 * GDN_HEADS
    small = jnp.zeros((D, HEAD_W), w_in_l.dtype).at[:, :2 * GDN_HEADS].set(w_in_l[:, o_b:o_d])
    return jnp.concatenate([w_in_l[:, :o_b], w_in_l[:, o_d:], small], axis=1).astype(bf16)


def _chunk_ffn(ffn_up_l, conv_w_l, conv_b_l, ffn_down_l, fc):
    D, two_f = ffn_up_l.shape
    F = two_f // 2
    n = F // fc
    pair = lambda m: jnp.concatenate([m[..., :F].reshape(m.shape[:-1] + (n, fc)),
                                      m[..., F:].reshape(m.shape[:-1] + (n, fc))], axis=-1)
    w_up_c = jnp.moveaxis(pair(ffn_up_l), -2, 0).astype(bf16)
    conv_rows = jnp.concatenate([conv_w_l.astype(f32), conv_b_l.astype(f32)[None]], axis=0)
    conv_c = jnp.moveaxis(pair(conv_rows), -2, 0)
    conv_c = jnp.zeros((n, 8, 2 * fc), f32).at[:, :FFN_CONV + 1].set(conv_c)
    w_down_c = ffn_down_l.reshape(n, fc, ffn_down_l.shape[1]).astype(bf16)
    return w_up_c, conv_c, w_down_c


def kernel(x, c, w_ada, b_ada, norm_mix, norm_ffn, w_in, gdn_conv_w, gdn_a_log, gdn_dt_bias, gdn_out_norm,
           diff_q_norm, diff_k_norm, diff_lambda_q1, diff_lambda_k1, diff_lambda_q2, diff_lambda_k2, diff_subln,
           rel_bias, w_out, ffn_up, ffn_conv_w, ffn_conv_b, ffn_down):
    B, T, D = x.shape
    depth = w_in.shape[0]
    mod = _ada_modulation(c, w_ada, b_ada).reshape(depth, B, 6, D)
    db = _bias_tiles(rel_bias, ATTN_TQ, ATTN_TK)
    for l in range(depth):
        proj = _in_projection(x, mod[l], norm_mix[l], _regroup_w_in(w_in[l]))
        y_gdn = _gdn_mixer(proj, gdn_conv_w[l], gdn_a_log[l], gdn_dt_bias[l], gdn_out_norm[l])
        lambda_init = 0.8 - 0.6 * math.exp(-0.3 * l)
        y_diff = _diff_attention(proj, db, diff_q_norm[l], diff_k_norm[l], diff_lambda_q1[l], diff_lambda_k1[l],
                                 diff_lambda_q2[l], diff_lambda_k2[l], diff_subln[l], lambda_init)
        w_up_c, conv_c, w_down_c = _chunk_ffn(ffn_up[l], ffn_conv_w[l], ffn_conv_b[l], ffn_down[l], FFN_CHUNK)
        x = _out_ffn(x, y_gdn, y_diff, mod[l], norm_ffn[l], w_out[l].astype(bf16), w_up_c, conv_c, w_down_c)
    return x
```

```python
import functools
import math

import numpy as np
import jax
import jax.numpy as jnp
from jax import lax
from jax.experimental import pallas as pl
from jax.experimental.pallas import tpu as pltpu

f32 = jnp.float32
bf16 = jnp.bfloat16

EPS = 1e-6
GDN_HEADS = 4
GDN_DK = 128
GDN_CONV = 4
GDN_CHUNK = 64
GDN_LOCKSTEP = 4
DIFF_HEADS = 4
DIFF_DH = 64
REL_BUCKETS = 32
REL_MAX_DIST = 128
FFN_CONV = 3
HEAD_W = 128
GROUP_W = GDN_HEADS * HEAD_W
NEG_BIG = -1e30
LOG2E = math.log2(math.e)
BF16_ROWS = 16

VMEM_LIMIT_BYTES = 56 * 1024 * 1024

COL_GQ, COL_GK, COL_GV, COL_GATE, COL_DQ, COL_DK, COL_DV, COL_BA = (
    0, GROUP_W, 2 * GROUP_W, 3 * GROUP_W, 4 * GROUP_W, 5 * GROUP_W, 6 * GROUP_W, 7 * GROUP_W)
PROJ_W = 7 * GROUP_W + HEAD_W

TOKEN_TILE = 512
ATTN_TQ = 256
ATTN_TK = 256
ATTN_GROUP = 8
ATTN_LOOKAHEAD = 4
SOFTMAX_DENOM_MIN = 2.0 ** -64
SOFTMAX_DENOM_MAX = 2.0 ** 80
FFN_CHUNK = 256
FFN_ROW_SPLIT = 2


def _bdot(a, b):
    return jnp.dot(a.astype(bf16), b.astype(bf16), preferred_element_type=f32)


def _bdot_nt(a, b):
    return lax.dot_general(a.astype(bf16), b.astype(bf16), (((1,), (1,)), ((), ())),
                           preferred_element_type=f32)


def _silu(x):
    return x * (1.0 / (1.0 + jnp.exp(-x)))


def _params(semantics):
    return pltpu.CompilerParams(dimension_semantics=semantics, vmem_limit_bytes=VMEM_LIMIT_BYTES)


def _const_spec(shape):
    nd = len(shape)
    return pl.BlockSpec(shape, lambda *_: (0,) * nd, pipeline_mode=pl.Buffered(1))


def _ada_kernel(c_ref, w_ref, b_ref, o_ref):
    ca = _silu(c_ref[...])
    o_ref[0] = _bdot(ca, w_ref[0]) + b_ref[0]


def _ada_modulation(c, w_ada, b_ada):
    L, D, N = w_ada.shape
    B = c.shape[0]
    tn = 1536
    return pl.pallas_call(
        _ada_kernel,
        out_shape=jax.ShapeDtypeStruct((L, B, N), f32),
        grid=(L, N // tn),
        in_specs=[
            pl.BlockSpec((B, D), lambda l, n: (0, 0)),
            pl.BlockSpec((1, D, tn), lambda l, n: (l, 0, n)),
            pl.BlockSpec((1, 1, tn), lambda l, n: (l, 0, n)),
        ],
        out_specs=pl.BlockSpec((1, B, tn), lambda l, n: (l, 0, n)),
        compiler_params=_params(("arbitrary", "arbitrary")),
        name="ada_modulation",
    )(c, w_ada, b_ada.reshape(L, 1, N))


def _bucket_of_distance(n):
    max_exact = REL_BUCKETS // 2
    nf = np.maximum(n, 1).astype(np.float32)
    large = max_exact + (np.log(nf / np.float32(max_exact)) / np.float32(math.log(REL_MAX_DIST / max_exact))
                         * np.float32(REL_BUCKETS - max_exact)).astype(np.int32)
    large = np.minimum(large, REL_BUCKETS - 1)
    return np.where(n < max_exact, n, large).astype(np.int32)


def _bias_bucket_tiles(tq, tk):
    assert tq == tk
    i = np.arange(tq)[None, :]
    j = np.arange(tk)[:, None]
    d0 = i - j
    diag = np.where(d0 >= 0, _bucket_of_distance(np.maximum(d0, 0)), -1)
    sub = _bucket_of_distance(tk + d0)
    assert _bucket_of_distance(np.array([tk + 1]))[0] == REL_BUCKETS - 1
    both = np.stack([diag, sub]).astype(np.int32)
    return np.concatenate([both, both], axis=2)


def _bias_tile_kernel(rel_ref, bk_ref, o_ref):
    h = pl.program_id(1)
    bk = bk_ref[0]
    last = rel_ref[REL_BUCKETS - 1, h]
    acc = jnp.where(bk < 0, NEG_BIG, 0.0).astype(f32)
    for b in range(REL_BUCKETS - 1):
        acc = jnp.where(bk == b, (rel_ref[b, h] - last) * LOG2E, acc)
    o_ref[0, 0] = acc


def _bias_tiles(rel_bias, tq, tk):
    buckets = jnp.asarray(_bias_bucket_tiles(tq, tk))
    return pl.pallas_call(
        _bias_tile_kernel,
        out_shape=jax.ShapeDtypeStruct((2, DIFF_HEADS, tk, 2 * tq), f32),
        grid=(2, DIFF_HEADS),
        in_specs=[
            pl.BlockSpec(memory_space=pltpu.SMEM),
            pl.BlockSpec((1, tk, 2 * tq), lambda k, h: (k, 0, 0)),
        ],
        out_specs=pl.BlockSpec((1, 1, tk, 2 * tq), lambda k, h: (k, h, 0, 0)),
        compiler_params=_params(("arbitrary", "arbitrary")),
        name="rel_bias_tiles",
    )(rel_bias.astype(f32), buckets)


def _modulated_norm(x, gain_row, mod_ref, shift_idx, scale_idx):
    ms = jnp.mean(x * x, axis=-1, keepdims=True)
    y = x * lax.rsqrt(ms + EPS)
    return y * (gain_row * (1.0 + mod_ref[0, scale_idx:scale_idx + 1, :])) + mod_ref[0, shift_idx:shift_idx + 1, :]


def _inproj_kernel(x_ref, mod_ref, g_ref, w_ref, o_ref, h_scr):
    h_scr[...] = _modulated_norm(x_ref[0], g_ref[...], mod_ref, 0, 1).astype(bf16)
    n = o_ref.shape[2]
    for c0 in range(0, n, GROUP_W):
        c1 = min(c0 + GROUP_W, n)
        o_ref[0, :, c0:c1] = jnp.dot(h_scr[...], w_ref[:, c0:c1], preferred_element_type=f32)


def _in_projection(x, mod_l, gain, w_perm):
    B, T, D = x.shape
    tm = TOKEN_TILE
    return pl.pallas_call(
        _inproj_kernel,
        out_shape=jax.ShapeDtypeStruct((B, T, PROJ_W), f32),
        grid=(B, T // tm),
        in_specs=[
            pl.BlockSpec((1, tm, D), lambda b, t: (b, t, 0)),
            pl.BlockSpec((1, 6, D), lambda b, t: (b, 0, 0)),
            _const_spec((1, D)),
            _const_spec((D, PROJ_W)),
        ],
        out_specs=pl.BlockSpec((1, tm, PROJ_W), lambda b, t: (b, t, 0)),
        scratch_shapes=[pltpu.VMEM((tm, D), bf16)],
        compiler_params=_params(("arbitrary", "arbitrary")),
        name="in_projection",
    )(x, mod_l, gain.reshape(1, D), w_perm)


def _shift_rows(x, prev8, s):
    xs = pltpu.roll(x, s, axis=0)
    fix = pltpu.roll(prev8, s, axis=0)
    row = lax.broadcasted_iota(jnp.int32, fix.shape, 0)
    head = jnp.where(row < s, fix, xs[:8])
    return jnp.concatenate([head, xs[8:]], axis=0)


def _causal_conv(x, prev8, w_ref, taps):
    acc = x * w_ref[taps - 1:taps, :]
    for s in range(1, taps):
        acc = acc + _shift_rows(x, prev8, s) * w_ref[taps - 1 - s:taps - s, :]
    return acc


def _block_lower_inverses(lows, base, top):
    n = lows[0].shape[0]
    row = lax.broadcasted_iota(jnp.int32, (n, n), 0)
    col = lax.broadcasted_iota(jnp.int32, (n, n), 1)
    same_block = lambda size: (row // size) == (col // size)
    eye = jnp.where(row == col, 1.0, 0.0).astype(f32)
    base_mask = same_block(base)
    lbs = [jnp.where(base_mask, low, 0.0) for low in lows]
    xs = [eye - lb for lb in lbs]
    lbs = [lb.astype(bf16) for lb in lbs]
    ps = [jnp.dot(lb, lb, preferred_element_type=f32) for lb in lbs]
    for _ in range(int(math.log2(base)) - 2):
        pbs = [p.astype(bf16) for p in ps]
        xps = [jnp.dot(jnp.concatenate([x.astype(bf16), pb], axis=0), pb, preferred_element_type=f32)
               for x, pb in zip(xs, pbs)]
        xs = [x + xp[:n] for x, xp in zip(xs, xps)]
        ps = [xp[n:] for xp in xps]
    xs = [x + _bdot(x, p) for x, p in zip(xs, ps)]
    size = base
    while size < top:
        off_mask = jnp.logical_and(same_block(2 * size), jnp.logical_not(same_block(size)))
        offs = [jnp.where(off_mask, low, 0.0).astype(bf16) for low in lows]
        xbs = [x.astype(bf16) for x in xs]
        ts = [jnp.dot(xb, off, preferred_element_type=f32) for xb, off in zip(xbs, offs)]
        ts = [jnp.dot(t.astype(bf16), xb, preferred_element_type=f32) for t, xb in zip(ts, xbs)]
        xs = [x - t for x, t in zip(xs, ts)]
        size *= 2
    return xs


def _gdn_kernel(qkv_ref, gate_ref, ba_ref, convw_ref, alog_ref, dt_ref, onorm_ref, o_ref,
                q_s, k_s, v_s, gc_s, beta_s, state_s, carry_s):
    tg = qkv_ref.shape[1]
    C = GDN_CHUNK

    @pl.when(pl.program_id(1) == 0)
    def _():
        state_s[...] = jnp.zeros_like(state_s)
        carry_s[...] = jnp.zeros_like(carry_s)

    for grp, dst in enumerate((q_s, k_s, v_s)):
        for h in range(GDN_HEADS):
            c0 = grp * GROUP_W + h * HEAD_W
            x = qkv_ref[0, :, c0:c0 + HEAD_W]
            y = _silu(_causal_conv(x, carry_s[:, c0:c0 + HEAD_W], convw_ref.at[:, c0:c0 + HEAD_W], GDN_CONV))
            if grp < 2:
                y = y * lax.rsqrt(jnp.sum(y * y, axis=-1, keepdims=True) + EPS)
            if grp == 0:
                y = y * (GDN_DK ** -0.5)
            dst[:, h * HEAD_W:(h + 1) * HEAD_W] = y
    carry_s[...] = qkv_ref[0, tg - 8:tg, :]

    ba = ba_ref[0]
    beta_s[...] = 1.0 / (1.0 + jnp.exp(-ba))
    z = ba + dt_ref[...]
    g = -jnp.exp(alog_ref[...]) * (jnp.maximum(z, 0.0) + jnp.log1p(jnp.exp(-jnp.abs(z))))
    rowc = lax.broadcasted_iota(jnp.int32, g.shape, 0) % C
    s = 1
    while s < C:
        g = g + jnp.where(rowc >= s, pltpu.roll(g, s, axis=0), 0.0)
        s *= 2
    gc_s[...] = g

    H = GDN_HEADS
    n = H * C
    ri = lax.broadcasted_iota(jnp.int32, (n, n), 0)
    ci = lax.broadcasted_iota(jnp.int32, (n, n), 1)
    same_head = (ri // C) == (ci // C)
    incl = jnp.logical_and(same_head, ri >= ci)
    strict = jnp.logical_and(same_head, ri > ci)
    onorm = onorm_ref[...]

    def chunk_operands(c):
        rows = slice(c * C, (c + 1) * C)
        gcc = gc_s[rows, :]
        gct = gcc.T
        betac = beta_s[rows, :]
        glast = gc_s[(c + 1) * C - 1:(c + 1) * C, :]
        ks, kbs, qs, rhs, qgs, kdts, gcols, grows, egls = [], [], [], [], [], [], [], [], []
        for h in range(H):
            cols = slice(h * HEAD_W, (h + 1) * HEAD_W)
            q = q_s[rows, cols]
            k = k_s[rows, cols]
            v = v_s[rows, cols]
            beta = betac[:, h:h + 1]
            gcol = gcc[:, H + h:H + h + 1]
            gl = glast[:, H + h:H + h + 1]
            eg = jnp.exp(gcol)
            kb = k * beta
            ks.append(k)
            kbs.append(kb)
            qs.append(q)
            rhs.append(jnp.concatenate([v * beta, kb * eg], axis=1))
            qgs.append((q * eg).astype(bf16))
            kdts.append((k * jnp.exp(gl - gcol)).T.astype(bf16))
            gcols.append(gcol)
            grows.append(gct[H + h:H + h + 1, :])
            egls.append(jnp.exp(gl))
        gcol_all = jnp.concatenate(gcols, axis=0)
        grow_all = jnp.concatenate(grows, axis=1)
        decay = jnp.exp(jnp.where(incl, gcol_all - grow_all, NEG_BIG))
        return dict(kbq=jnp.concatenate(kbs + qs, axis=0).astype(bf16), k_all=jnp.concatenate(ks, axis=0).astype(bf16),
                    rhs=jnp.concatenate(rhs, axis=0).astype(bf16), decay=decay, qg=qgs, kdt=kdts, egl=egls)

    chunks = []
    for c0 in range(0, tg // C, GDN_LOCKSTEP):
        ops = [chunk_operands(c) for c in range(c0, c0 + GDN_LOCKSTEP)]
        kqs = [lax.dot_general(o["kbq"], o["k_all"], (((1,), (1,)), ((), ())), preferred_element_type=f32)
               for o in ops]
        lows = [jnp.where(strict, kq[:n] * o["decay"], 0.0) for kq, o in zip(kqs, ops)]
        for kq, o in zip(kqs, ops):
            o["a"] = (kq[n:] * o["decay"]).astype(bf16)
        t_invs = _block_lower_inverses(lows, 16, C)
        for t_inv, o in zip(t_invs, ops):
            o["uw"] = jnp.dot(t_inv.astype(bf16), o["rhs"], preferred_element_type=f32)
        chunks.extend(ops)

    states = [state_s[h] for h in range(H)]
    for c, o in enumerate(chunks):
        rows = slice(c * C, (c + 1) * C)
        uw = o["uw"]
        wss = [jnp.dot(jnp.concatenate([uw[h * C:(h + 1) * C, HEAD_W:].astype(bf16), o["qg"][h]], axis=0),
                       states[h].astype(bf16), preferred_element_type=f32) for h in range(H)]
        v_news = [uw[h * C:(h + 1) * C, :HEAD_W] - wss[h][:C] for h in range(H)]
        states = [states[h] * o["egl"][h]
                  + jnp.dot(o["kdt"][h], v_news[h].astype(bf16), preferred_element_type=f32) for h in range(H)]
        o_all = (jnp.concatenate([ws[C:] for ws in wss], axis=0)
                 + jnp.dot(o["a"], jnp.concatenate(v_news, axis=0).astype(bf16), preferred_element_type=f32))
        for h in range(H):
            cols = slice(h * HEAD_W, (h + 1) * HEAD_W)
            oh = o_all[h * C:(h + 1) * C]
            on = oh * lax.rsqrt(jnp.mean(oh * oh, axis=-1, keepdims=True) + EPS) * onorm
            o_ref[0, rows, cols] = on * _silu(gate_ref[0, rows, cols])
    for h in range(H):
        state_s[h] = states[h]


def _gdn_mixer(proj, conv_w, a_log, dt_bias, out_norm):
    B, T, _ = proj.shape
    tg = TOKEN_TILE
    convw = jnp.zeros((8, 3 * GROUP_W), f32).at[:GDN_CONV].set(conv_w.astype(f32))
    lane_pad = lambda v: jnp.zeros((1, HEAD_W), f32).at[0, GDN_HEADS:2 * GDN_HEADS].set(v.astype(f32))
    return pl.pallas_call(
        _gdn_kernel,
        out_shape=jax.ShapeDtypeStruct((B, T, GROUP_W), f32),
        grid=(B, T // tg),
        in_specs=[
            pl.BlockSpec((1, tg, 3 * GROUP_W), lambda b, t: (b, t, 0)),
            pl.BlockSpec((1, tg, GROUP_W), lambda b, t: (b, t, COL_GATE // GROUP_W)),
            pl.BlockSpec((1, tg, HEAD_W), lambda b, t: (b, t, COL_BA // HEAD_W)),
            _const_spec((8, 3 * GROUP_W)),
            _const_spec((1, HEAD_W)),
            _const_spec((1, HEAD_W)),
            _const_spec((1, HEAD_W)),
        ],
        out_specs=pl.BlockSpec((1, tg, GROUP_W), lambda b, t: (b, t, 0)),
        scratch_shapes=[
            pltpu.VMEM((tg, GROUP_W), f32),
            pltpu.VMEM((tg, GROUP_W), f32),
            pltpu.VMEM((tg, GROUP_W), f32),
            pltpu.VMEM((tg, HEAD_W), f32),
            pltpu.VMEM((tg, HEAD_W), f32),
            pltpu.VMEM((GDN_HEADS, GDN_DK, HEAD_W), f32),
            pltpu.VMEM((8, 3 * GROUP_W), f32),
        ],
        compiler_params=_params(("arbitrary", "arbitrary")),
        name="gdn_mixer",
    )(proj, proj, proj, convw, lane_pad(a_log), lane_pad(dt_bias), out_norm.reshape(1, HEAD_W).astype(f32))


def _half_rmsnorm(x, gain_row):
    lane = lax.broadcasted_iota(jnp.int32, x.shape, 1)
    lo = lane < DIFF_DH
    x2 = x * x
    s_lo = jnp.sum(jnp.where(lo, x2, 0.0), axis=-1, keepdims=True)
    s_hi = jnp.sum(jnp.where(lo, 0.0, x2), axis=-1, keepdims=True)
    ms = jnp.where(lo, s_lo, s_hi) * (1.0 / DIFF_DH)
    return x * lax.rsqrt(ms + EPS) * gain_row


def _attn_kernel(q_ref, k_ref, v_ref, db_ref, qn_ref, kn_ref, lam_ref, subln_ref, o_ref,
                 k_s, vt_s, q_s, m_s, acc_s, *, lambda_init):
    tq = q_ref.shape[1]
    tk = db_ref.shape[2]
    T = k_ref.shape[1]
    dv = v_ref.shape[2]
    qi = pl.program_id(2)

    @pl.when(qi == 0)
    def _():
        ones_rows = jnp.where(lax.broadcasted_iota(jnp.int32, (vt_s.shape[1] - dv, tk), 0) == 0, 1.0, 0.0)

        def prep(i, carry):
            rows = pl.ds(pl.multiple_of(i * tk, tk), tk)
            k_s[i] = _half_rmsnorm(k_ref[0, rows, :], kn_ref[...]).astype(bf16)
            vt_s[i] = jnp.concatenate([v_ref[0, rows, :].T, ones_rows], axis=0).astype(bf16)
            return carry
        lax.fori_loop(0, T // tk, prep, 0)

    qt = (_half_rmsnorm(q_ref[0], qn_ref[...]) * (DIFF_DH ** -0.5 * LOG2E)).T
    dim = lax.broadcasted_iota(jnp.int32, qt.shape, 0)
    q_s[0] = jnp.where(dim < DIFF_DH, qt, 0.0).astype(bf16)
    q_s[1] = jnp.where(dim < DIFF_DH, 0.0, qt).astype(bf16)

    n_far = jnp.maximum(qi - 1, 0)
    n_grouped = (n_far // ATTN_GROUP) * ATTN_GROUP

    def unshifted(js, biases, acc):
        units = [(t, c) for t in range(len(js)) for c in range(2)]
        scores = {}
        for i in range(len(units) + ATTN_LOOKAHEAD):
            if i < len(units):
                t, c = units[i]
                s = jnp.dot(k_s[js[t]], q_s[c], preferred_element_type=f32)
                scores[i] = s if biases[t] is None else s + biases[t][:, c * tq:(c + 1) * tq]
            if i >= ATTN_LOOKAHEAD:
                t, c = units[i - ATTN_LOOKAHEAD]
                pv = jnp.dot(vt_s[js[t]], jnp.exp2(scores.pop(i - ATTN_LOOKAHEAD)).astype(bf16),
                             preferred_element_type=f32)
                acc[c] = pv if acc[c] is None else acc[c] + pv
        return acc

    @pl.when(qi == 0)
    def _():
        acc = unshifted([0], [db_ref[0, 0]], [None, None])
        for c in range(2):
            acc_s[c] = acc[c]

    for rem in range(ATTN_GROUP):
        @pl.when(jnp.logical_and(qi >= 1, n_far - n_grouped == rem))
        def _():
            acc = unshifted([qi, qi - 1] + [n_grouped + t for t in range(rem)],
                            [db_ref[0, 0], db_ref[1, 0]] + [None] * rem, [None, None])
            for c in range(2):
                acc_s[c] = acc[c]

    def far_group(i, carry):
        acc = unshifted([ATTN_GROUP * i + t for t in range(ATTN_GROUP)], [None] * ATTN_GROUP,
                        [acc_s[0], acc_s[1]])
        for c in range(2):
            acc_s[c] = acc[c]
        return carry
    lax.fori_loop(0, n_far // ATTN_GROUP, far_group, 0)

    denom = jnp.concatenate([acc_s[0, dv:dv + 1, :], acc_s[1, dv:dv + 1, :]], axis=1)
    trusted = jnp.logical_and(jnp.min(denom) >= SOFTMAX_DENOM_MIN, jnp.max(denom) <= SOFTMAX_DENOM_MAX)

    @pl.when(jnp.logical_not(trusted))
    def _():
        m_s[...] = jnp.full_like(m_s, NEG_BIG)
        acc_s[...] = jnp.zeros_like(acc_s)

        def online(js, biases):
            for c in range(2):
                ss = []
                for j, bias in zip(js, biases):
                    s = jnp.dot(k_s[j], q_s[c], preferred_element_type=f32)
                    ss.append(s if bias is None else s + bias[:, c * tq:(c + 1) * tq])
                m_old = m_s[c]
                m_new = m_old
                for s in ss:
                    m_new = jnp.maximum(m_new, jnp.max(s, axis=0, keepdims=True))
                acc = jnp.exp2(m_old - m_new) * acc_s[c]
                for j, s in zip(js, ss):
                    acc = acc + jnp.dot(vt_s[j], jnp.exp2(s - m_new).astype(bf16), preferred_element_type=f32)
                acc_s[c] = acc
                m_s[c] = m_new

        online([qi], [db_ref[0, 0]])

        @pl.when(qi >= 1)
        def _():
            online([qi - 1], [db_ref[1, 0]])

        def far_tile(j, carry):
            online([j], [None])
            return carry
        lax.fori_loop(0, n_far, far_tile, 0)

    lam_rows = lam_ref[...]
    lam = (jnp.exp(jnp.sum(lam_rows[0:1] * lam_rows[1:2], axis=-1, keepdims=True))
           - jnp.exp(jnp.sum(lam_rows[2:3] * lam_rows[3:4], axis=-1, keepdims=True)) + lambda_init)
    acc1 = acc_s[0]
    acc2 = acc_s[1]
    ot = acc1[:dv] * (1.0 / acc1[dv:dv + 1]) - lam * (acc2[:dv] * (1.0 / acc2[dv:dv + 1]))
    ont = ot * lax.rsqrt(jnp.mean(ot * ot, axis=0, keepdims=True) + EPS)
    o_ref[0] = ont.T * (subln_ref[...] * (1.0 - lambda_init))


def _diff_attention(proj, db, q_norm, k_norm, lq1, lk1, lq2, lk2, subln, lambda_init):
    B, T, _ = proj.shape
    tq, tk = ATTN_TQ, ATTN_TK
    two = lambda v: jnp.concatenate([v, v]).reshape(1, HEAD_W).astype(f32)
    lam_rows = jnp.zeros((8, DIFF_DH), f32).at[:4].set(jnp.stack([lq1, lk1, lq2, lk2]).astype(f32))
    qb, kb, vb = COL_DQ // HEAD_W, COL_DK // HEAD_W, COL_DV // HEAD_W
    return pl.pallas_call(
        functools.partial(_attn_kernel, lambda_init=lambda_init),
        out_shape=jax.ShapeDtypeStruct((B, T, GROUP_W), f32),
        grid=(B, DIFF_HEADS, T // tq),
        in_specs=[
            pl.BlockSpec((1, tq, HEAD_W), lambda b, h, i: (b, i, qb + h)),
            pl.BlockSpec((1, T, HEAD_W), lambda b, h, i: (b, 0, kb + h)),
            pl.BlockSpec((1, T, HEAD_W), lambda b, h, i: (b, 0, vb + h)),
            pl.BlockSpec((2, 1, tk, 2 * tq), lambda b, h, i: (0, h, 0, 0)),
            _const_spec((1, HEAD_W)),
            _const_spec((1, HEAD_W)),
            _const_spec((8, DIFF_DH)),
            _const_spec((1, HEAD_W)),
        ],
        out_specs=pl.BlockSpec((1, tq, HEAD_W), lambda b, h, i: (b, i, h)),
        scratch_shapes=[
            pltpu.VMEM((T // tk, tk, HEAD_W), bf16),
            pltpu.VMEM((T // tk, HEAD_W + BF16_ROWS, tk), bf16),
            pltpu.VMEM((2, HEAD_W, tq), bf16),
            pltpu.VMEM((2, 1, tq), f32),
            pltpu.VMEM((2, HEAD_W + BF16_ROWS, tq), f32),
        ],
        compiler_params=_params(("arbitrary", "arbitrary", "arbitrary")),
        name="diff_attention",
    )(proj, proj, proj, db, two(q_norm), two(k_norm), lam_rows, subln.reshape(1, HEAD_W).astype(f32))


def _outffn_kernel(x_ref, yg_ref, yd_ref, mod_ref, g_ref, wout_ref, wup_ref, cw_ref, wdown_ref, o_ref,
                   h_s, acc_s, carry_s):
    nchunks = wup_ref.shape[0]
    fc = wdown_ref.shape[1]

    @pl.when(pl.program_id(1) == 0)
    def _():
        carry_s[...] = jnp.zeros_like(carry_s)

    half = yg_ref.shape[2]
    y = (jnp.dot(yg_ref[0].astype(bf16), wout_ref[:half, :], preferred_element_type=f32)
         + jnp.dot(yd_ref[0].astype(bf16), wout_ref[half:, :], preferred_element_type=f32))
    x1 = x_ref[0] + mod_ref[0, 2:3, :] * y
    o_ref[0] = x1
    h_s[...] = _modulated_norm(x1, g_ref[...], mod_ref, 3, 4).astype(bf16)
    acc_s[...] = jnp.zeros_like(acc_s)
    tm = h_s.shape[0]

    tr = tm // FFN_ROW_SPLIT

    def up(j):
        return [jnp.dot(h_s[r0:r0 + tr, :], wup_ref[j], preferred_element_type=f32)
                for r0 in range(0, tm, tr)]

    us = up(0)
    for j in range(nchunks):
        us_next = up(j + 1) if j + 1 < nchunks else None
        cw = cw_ref.at[j]
        prev8 = carry_s[j]
        for i, u in enumerate(us):
            yc = _causal_conv(u, prev8, cw, FFN_CONV) + cw[FFN_CONV:FFN_CONV + 1, :]
            prev8 = u[tr - 8:, :]
            act = _silu(yc[:, fc:]) * yc[:, :fc]
            acc_s[i * tr:(i + 1) * tr, :] += jnp.dot(act.astype(bf16), wdown_ref[j], preferred_element_type=f32)
        carry_s[j] = prev8
        us = us_next
    o_ref[0] = o_ref[0] + mod_ref[0, 5:6, :] * acc_s[...]


def _out_ffn(x, y_gdn, y_diff, mod_l, gain, w_out, w_up_c, conv_c, w_down_c):
    B, T, D = x.shape
    tm = TOKEN_TILE
    nchunks, _, fc2 = w_up_c.shape
    fc = fc2 // 2
    return pl.pallas_call(
        _outffn_kernel,
        out_shape=jax.ShapeDtypeStruct((B, T, D), f32),
        grid=(B, T // tm),
        in_specs=[
            pl.BlockSpec((1, tm, D), lambda b, t: (b, t, 0)),
            pl.BlockSpec((1, tm, GROUP_W), lambda b, t: (b, t, 0)),
            pl.BlockSpec((1, tm, GROUP_W), lambda b, t: (b, t, 0)),
            pl.BlockSpec((1, 6, D), lambda b, t: (b, 0, 0)),
            _const_spec((1, D)),
            _const_spec((2 * GROUP_W, D)),
            _const_spec((nchunks, D, fc2)),
            _const_spec((nchunks, 8, fc2)),
            _const_spec((nchunks, fc, D)),
        ],
        out_specs=pl.BlockSpec((1, tm, D), lambda b, t: (b, t, 0)),
        scratch_shapes=[
            pltpu.VMEM((tm, D), bf16),
            pltpu.VMEM((tm, D), f32),
            pltpu.VMEM((nchunks, 8, fc2), f32),
        ],
        compiler_params=_params(("arbitrary", "arbitrary")),
        name="out_ffn",
    )(x, y_gdn, y_diff, mod_l, gain.reshape(1, D), w_out, w_up_c, conv_c, w_down_c)


def _regroup_w_in(w_in_l):
    D = w_in_l.shape[0]
    o_b = 4 * GROUP_W
    o_d = o_b + 2 * GDN_HEADS
    small = jnp.zeros((D, HEAD_W), w_in_l.dtype).at[:, :2 * GDN_HEADS].set(w_in_l[:, o_b:o_d])
    return jnp.concatenate([w_in_l[:, :o_b], w_in_l[:, o_d:], small], axis=1).astype(bf16)


def _chunk_ffn(ffn_up_l, conv_w_l, conv_b_l, ffn_down_l, fc):
    D, two_f = ffn_up_l.shape
    F = two_f // 2
    n = F // fc
    pair = lambda m: jnp.concatenate([m[..., :F].reshape(m.shape[:-1] + (n, fc)),
                                      m[..., F:].reshape(m.shape[:-1] + (n, fc))], axis=-1)
    w_up_c = jnp.moveaxis(pair(ffn_up_l), -2, 0).astype(bf16)
    conv_rows = jnp.concatenate([conv_w_l.astype(f32), conv_b_l.astype(f32)[None]], axis=0)
    conv_c = jnp.moveaxis(pair(conv_rows), -2, 0)
    conv_c = jnp.zeros((n, 8, 2 * fc), f32).at[:, :FFN_CONV + 1].set(conv_c)
    w_down_c = ffn_down_l.reshape(n, fc, ffn_down_l.shape[1]).astype(bf16)
    return w_up_c, conv_c, w_down_c


def kernel(x, c, w_ada, b_ada, norm_mix, norm_ffn, w_in, gdn_conv_w, gdn_a_log, gdn_dt_bias, gdn_out_norm,
           diff_q_norm, diff_k_norm, diff_lambda_q1, diff_lambda_k1, diff_lambda_q2, diff_lambda_k2, diff_subln,
           rel_bias, w_out, ffn_up, ffn_conv_w, ffn_conv_b, ffn_down):
    B, T, D = x.shape
    depth = w_in.shape[0]
    mod = _ada_modulation(c, w_ada, b_ada).reshape(depth, B, 6, D)
    db = _bias_tiles(rel_bias, ATTN_TQ, ATTN_TK)
    for l in range(depth):
        proj = _in_projection(x, mod[l], norm_mix[l], _regroup_w_in(w_in[l]))
        y_gdn = _gdn_mixer(proj, gdn_conv_w[l], gdn_a_log[l], gdn_dt_bias[l], gdn_out_norm[l])
        lambda_init = 0.8 - 0.6 * math.exp(-0.3 * l)
        y_diff = _diff_attention(proj, db, diff_q_norm[l], diff_k_norm[l], diff_lambda_q1[l], diff_lambda_k1[l],
                                 diff_lambda_q2[l], diff_lambda_k2[l], diff_subln[l], lambda_init)
        w_up_c, conv_c, w_down_c = _chunk_ffn(ffn_up[l], ffn_conv_w[l], ffn_conv_b[l], ffn_down[l], FFN_CHUNK)
        x = _out_ffn(x, y_gdn, y_diff, mod[l], norm_ffn[l], w_out[l].astype(bf16), w_up_c, conv_c, w_down_c)
    return x
```

```python
import functools
import math

import numpy as np
import jax
import jax.numpy as jnp
from jax import lax
from jax.experimental import pallas as pl
from jax.experimental.pallas import tpu as pltpu

f32 = jnp.float32
bf16 = jnp.bfloat16

EPS = 1e-6
GDN_HEADS = 4
GDN_DK = 128
GDN_CONV = 4
GDN_CHUNK = 64
GDN_LOCKSTEP = 4
DIFF_HEADS = 4
DIFF_DH = 64
REL_BUCKETS = 32
REL_MAX_DIST = 128
FFN_CONV = 3
HEAD_W = 128
GROUP_W = GDN_HEADS * HEAD_W
NEG_BIG = -1e30
LOG2E = math.log2(math.e)
BF16_ROWS = 16

VMEM_LIMIT_BYTES = 56 * 1024 * 1024

COL_GQ, COL_GK, COL_GV, COL_GATE, COL_DQ, COL_DK, COL_DV, COL_BA = (
    0, GROUP_W, 2 * GROUP_W, 3 * GROUP_W, 4 * GROUP_W, 5 * GROUP_W, 6 * GROUP_W, 7 * GROUP_W)
PROJ_W = 7 * GROUP_W + HEAD_W

TOKEN_TILE = 512
ATTN_TQ = 512
ATTN_TK = 256
ATTN_GROUP = 8
ATTN_LOOKAHEAD = 4
SOFTMAX_DENOM_MIN = 2.0 ** -64
SOFTMAX_DENOM_MAX = 2.0 ** 80
FFN_CHUNK = 256
FFN_ROW_SPLIT = 2


def _bdot(a, b):
    return jnp.dot(a.astype(bf16), b.astype(bf16), preferred_element_type=f32)


def _bdot_nt(a, b):
    return lax.dot_general(a.astype(bf16), b.astype(bf16), (((1,), (1,)), ((), ())),
                           preferred_element_type=f32)


def _silu(x):
    return x * (1.0 / (1.0 + jnp.exp(-x)))


def _params(semantics):
    return pltpu.CompilerParams(dimension_semantics=semantics, vmem_limit_bytes=VMEM_LIMIT_BYTES)


def _const_spec(shape):
    nd = len(shape)
    return pl.BlockSpec(shape, lambda *_: (0,) * nd, pipeline_mode=pl.Buffered(1))


def _ada_kernel(c_ref, w_ref, b_ref, o_ref):
    ca = _silu(c_ref[...])
    o_ref[0] = _bdot(ca, w_ref[0]) + b_ref[0]


def _ada_modulation(c, w_ada, b_ada):
    L, D, N = w_ada.shape
    B = c.shape[0]
    tn = 1536
    return pl.pallas_call(
        _ada_kernel,
        out_shape=jax.ShapeDtypeStruct((L, B, N), f32),
        grid=(L, N // tn),
        in_specs=[
            pl.BlockSpec((B, D), lambda l, n: (0, 0)),
            pl.BlockSpec((1, D, tn), lambda l, n: (l, 0, n)),
            pl.BlockSpec((1, 1, tn), lambda l, n: (l, 0, n)),
        ],
        out_specs=pl.BlockSpec((1, B, tn), lambda l, n: (l, 0, n)),
        compiler_params=_params(("arbitrary", "arbitrary")),
        name="ada_modulation",
    )(c, w_ada, b_ada.reshape(L, 1, N))


def _bucket_of_distance(n):
    max_exact = REL_BUCKETS // 2
    nf = np.maximum(n, 1).astype(np.float32)
    large = max_exact + (np.log(nf / np.float32(max_exact)) / np.float32(math.log(REL_MAX_DIST / max_exact))
                         * np.float32(REL_BUCKETS - max_exact)).astype(np.int32)
    large = np.minimum(large, REL_BUCKETS - 1)
    return np.where(n < max_exact, n, large).astype(np.int32)


def _bias_bucket_tiles(tq, tk):
    assert tq % tk == 0
    r = tq // tk
    i = np.arange(tq)[None, :]
    j = np.arange(tk)[:, None]
    tiles = []
    for t in range(r + 1):
        dist = i - j + (t - (r - 1)) * tk
        tiles.append(np.where(dist >= 0, _bucket_of_distance(np.maximum(dist, 0)), -1))
    assert _bucket_of_distance(np.array([tk + 1]))[0] == REL_BUCKETS - 1
    near = np.stack(tiles).astype(np.int32)
    return np.concatenate([near, near], axis=2)


def _bias_tile_kernel(rel_ref, bk_ref, o_ref):
    h = pl.program_id(1)
    bk = bk_ref[0]
    last = rel_ref[REL_BUCKETS - 1, h]
    acc = jnp.where(bk < 0, NEG_BIG, 0.0).astype(f32)
    for b in range(REL_BUCKETS - 1):
        acc = jnp.where(bk == b, (rel_ref[b, h] - last) * LOG2E, acc)
    o_ref[0, 0] = acc


def _bias_tiles(rel_bias, tq, tk):
    buckets = jnp.asarray(_bias_bucket_tiles(tq, tk))
    return pl.pallas_call(
        _bias_tile_kernel,
        out_shape=jax.ShapeDtypeStruct((buckets.shape[0], DIFF_HEADS, tk, 2 * tq), f32),
        grid=(buckets.shape[0], DIFF_HEADS),
        in_specs=[
            pl.BlockSpec(memory_space=pltpu.SMEM),
            pl.BlockSpec((1, tk, 2 * tq), lambda k, h: (k, 0, 0)),
        ],
        out_specs=pl.BlockSpec((1, 1, tk, 2 * tq), lambda k, h: (k, h, 0, 0)),
        compiler_params=_params(("arbitrary", "arbitrary")),
        name="rel_bias_tiles",
    )(rel_bias.astype(f32), buckets)


def _modulated_norm(x, gain_row, mod_ref, shift_idx, scale_idx):
    ms = jnp.mean(x * x, axis=-1, keepdims=True)
    y = x * lax.rsqrt(ms + EPS)
    return y * (gain_row * (1.0 + mod_ref[0, scale_idx:scale_idx + 1, :])) + mod_ref[0, shift_idx:shift_idx + 1, :]


def _inproj_kernel(x_ref, mod_ref, g_ref, w_ref, o_ref, h_scr):
    h_scr[...] = _modulated_norm(x_ref[0], g_ref[...], mod_ref, 0, 1).astype(bf16)
    n = o_ref.shape[2]
    for c0 in range(0, n, GROUP_W):
        c1 = min(c0 + GROUP_W, n)
        o_ref[0, :, c0:c1] = jnp.dot(h_scr[...], w_ref[:, c0:c1], preferred_element_type=f32)


def _in_projection(x, mod_l, gain, w_perm):
    B, T, D = x.shape
    tm = TOKEN_TILE
    return pl.pallas_call(
        _inproj_kernel,
        out_shape=jax.ShapeDtypeStruct((B, T, PROJ_W), f32),
        grid=(B, T // tm),
        in_specs=[
            pl.BlockSpec((1, tm, D), lambda b, t: (b, t, 0)),
            pl.BlockSpec((1, 6, D), lambda b, t: (b, 0, 0)),
            _const_spec((1, D)),
            _const_spec((D, PROJ_W)),
        ],
        out_specs=pl.BlockSpec((1, tm, PROJ_W), lambda b, t: (b, t, 0)),
        scratch_shapes=[pltpu.VMEM((tm, D), bf16)],
        compiler_params=_params(("arbitrary", "arbitrary")),
        name="in_projection",
    )(x, mod_l, gain.reshape(1, D), w_perm)


def _shift_rows(x, prev8, s):
    xs = pltpu.roll(x, s, axis=0)
    fix = pltpu.roll(prev8, s, axis=0)
    row = lax.broadcasted_iota(jnp.int32, fix.shape, 0)
    head = jnp.where(row < s, fix, xs[:8])
    return jnp.concatenate([head, xs[8:]], axis=0)


def _causal_conv(x, prev8, w_ref, taps):
    acc = x * w_ref[taps - 1:taps, :]
    for s in range(1, taps):
        acc = acc + _shift_rows(x, prev8, s) * w_ref[taps - 1 - s:taps - s, :]
    return acc


def _block_lower_inverses(lows, base, top):
    n = lows[0].shape[0]
    row = lax.broadcasted_iota(jnp.int32, (n, n), 0)
    col = lax.broadcasted_iota(jnp.int32, (n, n), 1)
    same_block = lambda size: (row // size) == (col // size)
    eye = jnp.where(row == col, 1.0, 0.0).astype(f32)
    base_mask = same_block(base)
    lbs = [jnp.where(base_mask, low, 0.0) for low in lows]
    xs = [eye - lb for lb in lbs]
    lbs = [lb.astype(bf16) for lb in lbs]
    ps = [jnp.dot(lb, lb, preferred_element_type=f32) for lb in lbs]
    for _ in range(int(math.log2(base)) - 2):
        pbs = [p.astype(bf16) for p in ps]
        xps = [jnp.dot(jnp.concatenate([x.astype(bf16), pb], axis=0), pb, preferred_element_type=f32)
               for x, pb in zip(xs, pbs)]
        xs = [x + xp[:n] for x, xp in zip(xs, xps)]
        ps = [xp[n:] for xp in xps]
    xs = [x + _bdot(x, p) for x, p in zip(xs, ps)]
    size = base
    while size < top:
        off_mask = jnp.logical_and(same_block(2 * size), jnp.logical_not(same_block(size)))
        offs = [jnp.where(off_mask, low, 0.0).astype(bf16) for low in lows]
        xbs = [x.astype(bf16) for x in xs]
        ts = [jnp.dot(xb, off, preferred_element_type=f32) for xb, off in zip(xbs, offs)]
        ts = [jnp.dot(t.astype(bf16), xb, preferred_element_type=f32) for t, xb in zip(ts, xbs)]
        xs = [x - t for x, t in zip(xs, ts)]
        size *= 2
    return xs


def _gdn_kernel(qkv_ref, gate_ref, ba_ref, convw_ref, alog_ref, dt_ref, onorm_ref, o_ref,
                q_s, k_s, v_s, gc_s, beta_s, state_s, carry_s):
    tg = qkv_ref.shape[1]
    C = GDN_CHUNK

    @pl.when(pl.program_id(1) == 0)
    def _():
        state_s[...] = jnp.zeros_like(state_s)
        carry_s[...] = jnp.zeros_like(carry_s)

    for grp, dst in enumerate((q_s, k_s, v_s)):
        for h in range(GDN_HEADS):
            c0 = grp * GROUP_W + h * HEAD_W
            x = qkv_ref[0, :, c0:c0 + HEAD_W]
            y = _silu(_causal_conv(x, carry_s[:, c0:c0 + HEAD_W], convw_ref.at[:, c0:c0 + HEAD_W], GDN_CONV))
            if grp < 2:
                y = y * lax.rsqrt(jnp.sum(y * y, axis=-1, keepdims=True) + EPS)
            if grp == 0:
                y = y * (GDN_DK ** -0.5)
            dst[:, h * HEAD_W:(h + 1) * HEAD_W] = y
    carry_s[...] = qkv_ref[0, tg - 8:tg, :]

    ba = ba_ref[0]
    beta_s[...] = 1.0 / (1.0 + jnp.exp(-ba))
    z = ba + dt_ref[...]
    g = -jnp.exp(alog_ref[...]) * (jnp.maximum(z, 0.0) + jnp.log1p(jnp.exp(-jnp.abs(z))))
    rowc = lax.broadcasted_iota(jnp.int32, g.shape, 0) % C
    s = 1
    while s < C:
        g = g + jnp.where(rowc >= s, pltpu.roll(g, s, axis=0), 0.0)
        s *= 2
    gc_s[...] = g

    H = GDN_HEADS
    n = H * C
    ri = lax.broadcasted_iota(jnp.int32, (n, n), 0)
    ci = lax.broadcasted_iota(jnp.int32, (n, n), 1)
    same_head = (ri // C) == (ci // C)
    incl = jnp.logical_and(same_head, ri >= ci)
    strict = jnp.logical_and(same_head, ri > ci)
    onorm = onorm_ref[...]

    def chunk_operands(c):
        rows = slice(c * C, (c + 1) * C)
        gcc = gc_s[rows, :]
        gct = gcc.T
        betac = beta_s[rows, :]
        glast = gc_s[(c + 1) * C - 1:(c + 1) * C, :]
        ks, kbs, qs, rhs, qgs, kdts, gcols, grows, egls = [], [], [], [], [], [], [], [], []
        for h in range(H):
            cols = slice(h * HEAD_W, (h + 1) * HEAD_W)
            q = q_s[rows, cols]
            k = k_s[rows, cols]
            v = v_s[rows, cols]
            beta = betac[:, h:h + 1]
            gcol = gcc[:, H + h:H + h + 1]
            gl = glast[:, H + h:H + h + 1]
            eg = jnp.exp(gcol)
            kb = k * beta
            ks.append(k)
            kbs.append(kb)
            qs.append(q)
            rhs.append(jnp.concatenate([v * beta, kb * eg], axis=1))
            qgs.append((q * eg).astype(bf16))
            kdts.append((k * jnp.exp(gl - gcol)).T.astype(bf16))
            gcols.append(gcol)
            grows.append(gct[H + h:H + h + 1, :])
            egls.append(jnp.exp(gl))
        gcol_all = jnp.concatenate(gcols, axis=0)
        grow_all = jnp.concatenate(grows, axis=1)
        decay = jnp.exp(jnp.where(incl, gcol_all - grow_all, NEG_BIG))
        return dict(kbq=jnp.concatenate(kbs + qs, axis=0).astype(bf16), k_all=jnp.concatenate(ks, axis=0).astype(bf16),
                    rhs=jnp.concatenate(rhs, axis=0).astype(bf16), decay=decay, qg=qgs, kdt=kdts, egl=egls)

    chunks = []
    for c0 in range(0, tg // C, GDN_LOCKSTEP):
        ops = [chunk_operands(c) for c in range(c0, c0 + GDN_LOCKSTEP)]
        kqs = [lax.dot_general(o["kbq"], o["k_all"], (((1,), (1,)), ((), ())), preferred_element_type=f32)
               for o in ops]
        lows = [jnp.where(strict, kq[:n] * o["decay"], 0.0) for kq, o in zip(kqs, ops)]
        for kq, o in zip(kqs, ops):
            o["a"] = (kq[n:] * o["decay"]).astype(bf16)
        t_invs = _block_lower_inverses(lows, 16, C)
        for t_inv, o in zip(t_invs, ops):
            o["uw"] = jnp.dot(t_inv.astype(bf16), o["rhs"], preferred_element_type=f32)
        chunks.extend(ops)

    states = [state_s[h] for h in range(H)]
    for c, o in enumerate(chunks):
        rows = slice(c * C, (c + 1) * C)
        uw = o["uw"]
        wss = [jnp.dot(jnp.concatenate([uw[h * C:(h + 1) * C, HEAD_W:].astype(bf16), o["qg"][h]], axis=0),
                       states[h].astype(bf16), preferred_element_type=f32) for h in range(H)]
        v_news = [uw[h * C:(h + 1) * C, :HEAD_W] - wss[h][:C] for h in range(H)]
        states = [states[h] * o["egl"][h]
                  + jnp.dot(o["kdt"][h], v_news[h].astype(bf16), preferred_element_type=f32) for h in range(H)]
        o_all = (jnp.concatenate([ws[C:] for ws in wss], axis=0)
                 + jnp.dot(o["a"], jnp.concatenate(v_news, axis=0).astype(bf16), preferred_element_type=f32))
        for h in range(H):
            cols = slice(h * HEAD_W, (h + 1) * HEAD_W)
            oh = o_all[h * C:(h + 1) * C]
            on = oh * lax.rsqrt(jnp.mean(oh * oh, axis=-1, keepdims=True) + EPS) * onorm
            o_ref[0, rows, cols] = on * _silu(gate_ref[0, rows, cols])
    for h in range(H):
        state_s[h] = states[h]


def _gdn_mixer(proj, conv_w, a_log, dt_bias, out_norm):
    B, T, _ = proj.shape
    tg = TOKEN_TILE
    convw = jnp.zeros((8, 3 * GROUP_W), f32).at[:GDN_CONV].set(conv_w.astype(f32))
    lane_pad = lambda v: jnp.zeros((1, HEAD_W), f32).at[0, GDN_HEADS:2 * GDN_HEADS].set(v.astype(f32))
    return pl.pallas_call(
        _gdn_kernel,
        out_shape=jax.ShapeDtypeStruct((B, T, GROUP_W), f32),
        grid=(B, T // tg),
        in_specs=[
            pl.BlockSpec((1, tg, 3 * GROUP_W), lambda b, t: (b, t, 0)),
            pl.BlockSpec((1, tg, GROUP_W), lambda b, t: (b, t, COL_GATE // GROUP_W)),
            pl.BlockSpec((1, tg, HEAD_W), lambda b, t: (b, t, COL_BA // HEAD_W)),
            _const_spec((8, 3 * GROUP_W)),
            _const_spec((1, HEAD_W)),
            _const_spec((1, HEAD_W)),
            _const_spec((1, HEAD_W)),
        ],
        out_specs=pl.BlockSpec((1, tg, GROUP_W), lambda b, t: (b, t, 0)),
        scratch_shapes=[
            pltpu.VMEM((tg, GROUP_W), f32),
            pltpu.VMEM((tg, GROUP_W), f32),
            pltpu.VMEM((tg, GROUP_W), f32),
            pltpu.VMEM((tg, HEAD_W), f32),
            pltpu.VMEM((tg, HEAD_W), f32),
            pltpu.VMEM((GDN_HEADS, GDN_DK, HEAD_W), f32),
            pltpu.VMEM((8, 3 * GROUP_W), f32),
        ],
        compiler_params=_params(("arbitrary", "arbitrary")),
        name="gdn_mixer",
    )(proj, proj, proj, convw, lane_pad(a_log), lane_pad(dt_bias), out_norm.reshape(1, HEAD_W).astype(f32))


def _half_rmsnorm(x, gain_row):
    lane = lax.broadcasted_iota(jnp.int32, x.shape, 1)
    lo = lane < DIFF_DH
    x2 = x * x
    s_lo = jnp.sum(jnp.where(lo, x2, 0.0), axis=-1, keepdims=True)
    s_hi = jnp.sum(jnp.where(lo, 0.0, x2), axis=-1, keepdims=True)
    ms = jnp.where(lo, s_lo, s_hi) * (1.0 / DIFF_DH)
    return x * lax.rsqrt(ms + EPS) * gain_row


def _attn_kernel(q_ref, k_ref, v_ref, db_ref, qn_ref, kn_ref, lam_ref, subln_ref, o_ref,
                 k_s, vt_s, q_s, m_s, acc_s, *, lambda_init):
    tq = q_ref.shape[1]
    tk = db_ref.shape[2]
    T = k_ref.shape[1]
    dv = v_ref.shape[2]
    qi = pl.program_id(2)

    @pl.when(qi == 0)
    def _():
        ones_rows = jnp.where(lax.broadcasted_iota(jnp.int32, (vt_s.shape[1] - dv, tk), 0) == 0, 1.0, 0.0)

        def prep(i, carry):
            rows = pl.ds(pl.multiple_of(i * tk, tk), tk)
            k_s[i] = _half_rmsnorm(k_ref[0, rows, :], kn_ref[...]).astype(bf16)
            vt_s[i] = jnp.concatenate([v_ref[0, rows, :].T, ones_rows], axis=0).astype(bf16)
            return carry
        lax.fori_loop(0, T // tk, prep, 0)

    qt = (_half_rmsnorm(q_ref[0], qn_ref[...]) * (DIFF_DH ** -0.5 * LOG2E)).T
    dim = lax.broadcasted_iota(jnp.int32, qt.shape, 0)
    q_s[0] = jnp.where(dim < DIFF_DH, qt, 0.0).astype(bf16)
    q_s[1] = jnp.where(dim < DIFF_DH, 0.0, qt).astype(bf16)

    r = tq // tk
    n_near = db_ref.shape[0]
    near_tiles = [r * (qi + 1) - 1 - t for t in range(n_near)]
    near_bias = [db_ref[t, 0] for t in range(n_near)]
    n_far = jnp.maximum(r * qi - 1, 0)
    n_grouped = (n_far // ATTN_GROUP) * ATTN_GROUP

    def unshifted(js, biases, acc):
        units = [(t, c) for t in range(len(js)) for c in range(2)]
        scores = {}
        for i in range(len(units) + ATTN_LOOKAHEAD):
            if i < len(units):
                t, c = units[i]
                s = jnp.dot(k_s[js[t]], q_s[c], preferred_element_type=f32)
                scores[i] = s if biases[t] is None else s + biases[t][:, c * tq:(c + 1) * tq]
            if i >= ATTN_LOOKAHEAD:
                t, c = units[i - ATTN_LOOKAHEAD]
                pv = jnp.dot(vt_s[js[t]], jnp.exp2(scores.pop(i - ATTN_LOOKAHEAD)).astype(bf16),
                             preferred_element_type=f32)
                acc[c] = pv if acc[c] is None else acc[c] + pv
        return acc

    @pl.when(qi == 0)
    def _():
        acc = unshifted(near_tiles[:r], near_bias[:r], [None, None])
        for c in range(2):
            acc_s[c] = acc[c]

    for rem in range(ATTN_GROUP):
        @pl.when(jnp.logical_and(qi >= 1, n_far - n_grouped == rem))
        def _():
            acc = unshifted(near_tiles + [n_grouped + t for t in range(rem)],
                            near_bias + [None] * rem, [None, None])
            for c in range(2):
                acc_s[c] = acc[c]

    def far_group(i, carry):
        acc = unshifted([ATTN_GROUP * i + t for t in range(ATTN_GROUP)], [None] * ATTN_GROUP,
                        [acc_s[0], acc_s[1]])
        for c in range(2):
            acc_s[c] = acc[c]
        return carry
    lax.fori_loop(0, n_far // ATTN_GROUP, far_group, 0)

    denom = jnp.concatenate([acc_s[0, dv:dv + 1, :], acc_s[1, dv:dv + 1, :]], axis=1)
    trusted = jnp.logical_and(jnp.min(denom) >= SOFTMAX_DENOM_MIN, jnp.max(denom) <= SOFTMAX_DENOM_MAX)

    @pl.when(jnp.logical_not(trusted))
    def _():
        m_s[...] = jnp.full_like(m_s, NEG_BIG)
        acc_s[...] = jnp.zeros_like(acc_s)

        def online(js, biases):
            for c in range(2):
                ss = []
                for j, bias in zip(js, biases):
                    s = jnp.dot(k_s[j], q_s[c], preferred_element_type=f32)
                    ss.append(s if bias is None else s + bias[:, c * tq:(c + 1) * tq])
                m_old = m_s[c]
                m_new = m_old
                for s in ss:
                    m_new = jnp.maximum(m_new, jnp.max(s, axis=0, keepdims=True))
                acc = jnp.exp2(m_old - m_new) * acc_s[c]
                for j, s in zip(js, ss):
                    acc = acc + jnp.dot(vt_s[j], jnp.exp2(s - m_new).astype(bf16), preferred_element_type=f32)
                acc_s[c] = acc
                m_s[c] = m_new

        for t in range(r):
            online([near_tiles[t]], [near_bias[t]])

        @pl.when(qi >= 1)
        def _():
            online([near_tiles[r]], [near_bias[r]])

        def far_tile(j, carry):
            online([j], [None])
            return carry
        lax.fori_loop(0, n_far, far_tile, 0)

    lam_rows = lam_ref[...]
    lam = (jnp.exp(jnp.sum(lam_rows[0:1] * lam_rows[1:2], axis=-1, keepdims=True))
           - jnp.exp(jnp.sum(lam_rows[2:3] * lam_rows[3:4], axis=-1, keepdims=True)) + lambda_init)
    acc1 = acc_s[0]
    acc2 = acc_s[1]
    ot = acc1[:dv] * (1.0 / acc1[dv:dv + 1]) - lam * (acc2[:dv] * (1.0 / acc2[dv:dv + 1]))
    ont = ot * lax.rsqrt(jnp.mean(ot * ot, axis=0, keepdims=True) + EPS)
    o_ref[0] = ont.T * (subln_ref[...] * (1.0 - lambda_init))


def _diff_attention(proj, db, q_norm, k_norm, lq1, lk1, lq2, lk2, subln, lambda_init):
    B, T, _ = proj.shape
    tq, tk = ATTN_TQ, ATTN_TK
    two = lambda v: jnp.concatenate([v, v]).reshape(1, HEAD_W).astype(f32)
    lam_rows = jnp.zeros((8, DIFF_DH), f32).at[:4].set(jnp.stack([lq1, lk1, lq2, lk2]).astype(f32))
    qb, kb, vb = COL_DQ // HEAD_W, COL_DK // HEAD_W, COL_DV // HEAD_W
    return pl.pallas_call(
        functools.partial(_attn_kernel, lambda_init=lambda_init),
        out_shape=jax.ShapeDtypeStruct((B, T, GROUP_W), f32),
        grid=(B, DIFF_HEADS, T // tq),
        in_specs=[
            pl.BlockSpec((1, tq, HEAD_W), lambda b, h, i: (b, i, qb + h)),
            pl.BlockSpec((1, T, HEAD_W), lambda b, h, i: (b, 0, kb + h)),
            pl.BlockSpec((1, T, HEAD_W), lambda b, h, i: (b, 0, vb + h)),
            pl.BlockSpec((tq // tk + 1, 1, tk, 2 * tq), lambda b, h, i: (0, h, 0, 0)),
            _const_spec((1, HEAD_W)),
            _const_spec((1, HEAD_W)),
            _const_spec((8, DIFF_DH)),
            _const_spec((1, HEAD_W)),
        ],
        out_specs=pl.BlockSpec((1, tq, HEAD_W), lambda b, h, i: (b, i, h)),
        scratch_shapes=[
            pltpu.VMEM((T // tk, tk, HEAD_W), bf16),
            pltpu.VMEM((T // tk, HEAD_W + BF16_ROWS, tk), bf16),
            pltpu.VMEM((2, HEAD_W, tq), bf16),
            pltpu.VMEM((2, 1, tq), f32),
            pltpu.VMEM((2, HEAD_W + BF16_ROWS, tq), f32),
        ],
        compiler_params=_params(("arbitrary", "arbitrary", "arbitrary")),
        name="diff_attention",
    )(proj, proj, proj, db, two(q_norm), two(k_norm), lam_rows, subln.reshape(1, HEAD_W).astype(f32))


def _outffn_kernel(x_ref, yg_ref, yd_ref, mod_ref, g_ref, wout_ref, wup_ref, cw_ref, wdown_ref, o_ref,
                   h_s, acc_s, carry_s):
    nchunks = wup_ref.shape[0]
    fc = wdown_ref.shape[1]

    @pl.when(pl.program_id(1) == 0)
    def _():
        carry_s[...] = jnp.zeros_like(carry_s)

    half = yg_ref.shape[2]
    y = (jnp.dot(yg_ref[0].astype(bf16), wout_ref[:half, :], preferred_element_type=f32)
         + jnp.dot(yd_ref[0].astype(bf16), wout_ref[half:, :], preferred_element_type=f32))
    x1 = x_ref[0] + mod_ref[0, 2:3, :] * y
    o_ref[0] = x1
    h_s[...] = _modulated_norm(x1, g_ref[...], mod_ref, 3, 4).astype(bf16)
    acc_s[...] = jnp.zeros_like(acc_s)
    tm = h_s.shape[0]

    tr = tm // FFN_ROW_SPLIT

    def up(j):
        return [jnp.dot(h_s[r0:r0 + tr, :], wup_ref[j], preferred_element_type=f32)
                for r0 in range(0, tm, tr)]

    us = up(0)
    for j in range(nchunks):
        us_next = up(j + 1) if j + 1 < nchunks else None
        cw = cw_ref.at[j]
        prev8 = carry_s[j]
        for i, u in enumerate(us):
            yc = _causal_conv(u, prev8, cw, FFN_CONV) + cw[FFN_CONV:FFN_CONV + 1, :]
            prev8 = u[tr - 8:, :]
            act = _silu(yc[:, fc:]) * yc[:, :fc]
            acc_s[i * tr:(i + 1) * tr, :] += jnp.dot(act.astype(bf16), wdown_ref[j], preferred_element_type=f32)
        carry_s[j] = prev8
        us = us_next
    o_ref[0] = o_ref[0] + mod_ref[0, 5:6, :] * acc_s[...]


def _out_ffn(x, y_gdn, y_diff, mod_l, gain, w_out, w_up_c, conv_c, w_down_c):
    B, T, D = x.shape
    tm = TOKEN_TILE
    nchunks, _, fc2 = w_up_c.shape
    fc = fc2 // 2
    return pl.pallas_call(
        _outffn_kernel,
        out_shape=jax.ShapeDtypeStruct((B, T, D), f32),
        grid=(B, T // tm),
        in_specs=[
            pl.BlockSpec((1, tm, D), lambda b, t: (b, t, 0)),
            pl.BlockSpec((1, tm, GROUP_W), lambda b, t: (b, t, 0)),
            pl.BlockSpec((1, tm, GROUP_W), lambda b, t: (b, t, 0)),
            pl.BlockSpec((1, 6, D), lambda b, t: (b, 0, 0)),
            _const_spec((1, D)),
            _const_spec((2 * GROUP_W, D)),
            _const_spec((nchunks, D, fc2)),
            _const_spec((nchunks, 8, fc2)),
            _const_spec((nchunks, fc, D)),
        ],
        out_specs=pl.BlockSpec((1, tm, D), lambda b, t: (b, t, 0)),
        scratch_shapes=[
            pltpu.VMEM((tm, D), bf16),
            pltpu.VMEM((tm, D), f32),
            pltpu.VMEM((nchunks, 8, fc2), f32),
        ],
        compiler_params=_params(("arbitrary", "arbitrary")),
        name="out_ffn",
    )(x, y_gdn, y_diff, mod_l, gain.reshape(1, D), w_out, w_up_c, conv_c, w_down_c)


def _regroup_w_in(w_in_l):
    D = w_in_l.shape[0]
    o_b = 4 * GROUP_W
    o_d = o_b + 2 * GDN_HEADS
    small = jnp.zeros((D, HEAD_W), w_in_l.dtype).at[:, :2 * GDN_HEADS].set(w_in_l[:, o_b:o_d])
    return jnp.concatenate([w_in_l[:, :o_b], w_in_l[:, o_d:], small], axis=1).astype(bf16)


def _chunk_ffn(ffn_up_l, conv_w_l, conv_b_l, ffn_down_l, fc):
    D, two_f = ffn_up_l.shape
    F = two_f // 2
    n = F // fc
    pair = lambda m: jnp.concatenate([m[..., :F].reshape(m.shape[:-1] + (n, fc)),
                                      m[..., F:].reshape(m.shape[:-1] + (n, fc))], axis=-1)
    w_up_c = jnp.moveaxis(pair(ffn_up_l), -2, 0).astype(bf16)
    conv_rows = jnp.concatenate([conv_w_l.astype(f32), conv_b_l.astype(f32)[None]], axis=0)
    conv_c = jnp.moveaxis(pair(conv_rows), -2, 0)
    conv_c = jnp.zeros((n, 8, 2 * fc), f32).at[:, :FFN_CONV + 1].set(conv_c)
    w_down_c = ffn_down_l.reshape(n, fc, ffn_down_l.shape[1]).astype(bf16)
    return w_up_c, conv_c, w_down_c


def kernel(x, c, w_ada, b_ada, norm_mix, norm_ffn, w_in, gdn_conv_w, gdn_a_log, gdn_dt_bias, gdn_out_norm,
           diff_q_norm, diff_k_norm, diff_lambda_q1, diff_lambda_k1, diff_lambda_q2, diff_lambda_k2, diff_subln,
           rel_bias, w_out, ffn_up, ffn_conv_w, ffn_conv_b, ffn_down):
    B, T, D = x.shape
    depth = w_in.shape[0]
    mod = _ada_modulation(c, w_ada, b_ada).reshape(depth, B, 6, D)
    db = _bias_tiles(rel_bias, ATTN_TQ, ATTN_TK)
    for l in range(depth):
        proj = _in_projection(x, mod[l], norm_mix[l], _regroup_w_in(w_in[l]))
        y_gdn = _gdn_mixer(proj, gdn_conv_w[l], gdn_a_log[l], gdn_dt_bias[l], gdn_out_norm[l])
        lambda_init = 0.8 - 0.6 * math.exp(-0.3 * l)
        y_diff = _diff_attention(proj, db, diff_q_norm[l], diff_k_norm[l], diff_lambda_q1[l], diff_lambda_k1[l],
                                 diff_lambda_q2[l], diff_lambda_k2[l], diff_subln[l], lambda_init)
        w_up_c, conv_c, w_down_c = _chunk_ffn(ffn_up[l], ffn_conv_w[l], ffn_conv_b[l], ffn_down[l], FFN_CHUNK)
        x = _out_ffn(x, y_gdn, y_diff, mod[l], norm_ffn[l], w_out[l].astype(bf16), w_up_c, conv_c, w_down_c)
    return x
```

```python
import functools
import math

import numpy as np
import jax
import jax.numpy as jnp
from jax import lax
from jax.experimental import pallas as pl
from jax.experimental.pallas import tpu as pltpu

f32 = jnp.float32
bf16 = jnp.bfloat16

EPS = 1e-6
GDN_HEADS = 4
GDN_DK = 128
GDN_CONV = 4
GDN_CHUNK = 64
GDN_LOCKSTEP = 4
DIFF_HEADS = 4
DIFF_DH = 64
REL_BUCKETS = 32
REL_MAX_DIST = 128
FFN_CONV = 3
HEAD_W = 128
GROUP_W = GDN_HEADS * HEAD_W
NEG_BIG = -1e30
LOG2E = math.log2(math.e)
BF16_ROWS = 16

VMEM_LIMIT_BYTES = 56 * 1024 * 1024

COL_GQ, COL_GK, COL_GV, COL_GATE, COL_DQ, COL_DK, COL_DV, COL_BA = (
    0, GROUP_W, 2 * GROUP_W, 3 * GROUP_W, 4 * GROUP_W, 5 * GROUP_W, 6 * GROUP_W, 7 * GROUP_W)
PROJ_W = 7 * GROUP_W + HEAD_W

TOKEN_TILE = 512
GDN_TOKEN_TILE = 1024
ATTN_TQ = 512
ATTN_TK = 256
ATTN_PREP_TILES = 4
ATTN_GROUP = 8
ATTN_LOOKAHEAD = 4
SOFTMAX_DENOM_MIN = 2.0 ** -64
SOFTMAX_DENOM_MAX = 2.0 ** 80
FFN_CHUNK = 256
FFN_ROW_SPLIT = 2


def _bdot(a, b):
    return jnp.dot(a.astype(bf16), b.astype(bf16), preferred_element_type=f32)


def _bdot_nt(a, b):
    return lax.dot_general(a.astype(bf16), b.astype(bf16), (((1,), (1,)), ((), ())),
                           preferred_element_type=f32)


def _silu(x):
    return x * (1.0 / (1.0 + jnp.exp(-x)))


def _params(semantics):
    return pltpu.CompilerParams(dimension_semantics=semantics, vmem_limit_bytes=VMEM_LIMIT_BYTES)


def _const_spec(shape):
    nd = len(shape)
    return pl.BlockSpec(shape, lambda *_: (0,) * nd, pipeline_mode=pl.Buffered(1))


def _ada_kernel(c_ref, w_ref, b_ref, o_ref):
    ca = _silu(c_ref[...])
    o_ref[0] = _bdot(ca, w_ref[0]) + b_ref[0]


def _ada_modulation(c, w_ada, b_ada):
    L, D, N = w_ada.shape
    B = c.shape[0]
    tn = 1536
    return pl.pallas_call(
        _ada_kernel,
        out_shape=jax.ShapeDtypeStruct((L, B, N), f32),
        grid=(L, N // tn),
        in_specs=[
            pl.BlockSpec((B, D), lambda l, n: (0, 0)),
            pl.BlockSpec((1, D, tn), lambda l, n: (l, 0, n)),
            pl.BlockSpec((1, 1, tn), lambda l, n: (l, 0, n)),
        ],
        out_specs=pl.BlockSpec((1, B, tn), lambda l, n: (l, 0, n)),
        compiler_params=_params(("arbitrary", "arbitrary")),
        name="ada_modulation",
    )(c, w_ada, b_ada.reshape(L, 1, N))


def _bucket_of_distance(n):
    max_exact = REL_BUCKETS // 2
    nf = np.maximum(n, 1).astype(np.float32)
    large = max_exact + (np.log(nf / np.float32(max_exact)) / np.float32(math.log(REL_MAX_DIST / max_exact))
                         * np.float32(REL_BUCKETS - max_exact)).astype(np.int32)
    large = np.minimum(large, REL_BUCKETS - 1)
    return np.where(n < max_exact, n, large).astype(np.int32)


def _bias_bucket_tiles(tq, tk):
    assert tq % tk == 0
    r = tq // tk
    i = np.arange(tq)[None, :]
    j = np.arange(tk)[:, None]
    tiles = []
    for t in range(r + 1):
        dist = i - j + (t - (r - 1)) * tk
        tiles.append(np.where(dist >= 0, _bucket_of_distance(np.maximum(dist, 0)), -1))
    assert _bucket_of_distance(np.array([tk + 1]))[0] == REL_BUCKETS - 1
    near = np.stack(tiles).astype(np.int32)
    return np.concatenate([near, near], axis=2)


def _bias_tile_kernel(rel_ref, bk_ref, o_ref):
    h = pl.program_id(1)
    bk = bk_ref[0]
    last = rel_ref[REL_BUCKETS - 1, h]
    acc = jnp.where(bk < 0, NEG_BIG, 0.0).astype(f32)
    for b in range(REL_BUCKETS - 1):
        acc = jnp.where(bk == b, (rel_ref[b, h] - last) * LOG2E, acc)
    o_ref[0, 0] = acc


def _bias_tiles(rel_bias, tq, tk):
    buckets = jnp.asarray(_bias_bucket_tiles(tq, tk))
    return pl.pallas_call(
        _bias_tile_kernel,
        out_shape=jax.ShapeDtypeStruct((buckets.shape[0], DIFF_HEADS, tk, 2 * tq), f32),
        grid=(buckets.shape[0], DIFF_HEADS),
        in_specs=[
            pl.BlockSpec(memory_space=pltpu.SMEM),
            pl.BlockSpec((1, tk, 2 * tq), lambda k, h: (k, 0, 0)),
        ],
        out_specs=pl.BlockSpec((1, 1, tk, 2 * tq), lambda k, h: (k, h, 0, 0)),
        compiler_params=_params(("arbitrary", "arbitrary")),
        name="rel_bias_tiles",
    )(rel_bias.astype(f32), buckets)


def _modulated_norm(x, gain_row, mod_ref, shift_idx, scale_idx):
    ms = jnp.mean(x * x, axis=-1, keepdims=True)
    y = x * lax.rsqrt(ms + EPS)
    return y * (gain_row * (1.0 + mod_ref[0, scale_idx:scale_idx + 1, :])) + mod_ref[0, shift_idx:shift_idx + 1, :]


def _inproj_kernel(x_ref, mod_ref, g_ref, w_ref, o_ref, h_scr):
    h_scr[...] = _modulated_norm(x_ref[0], g_ref[...], mod_ref, 0, 1).astype(bf16)
    n = o_ref.shape[2]
    for c0 in range(0, n, GROUP_W):
        c1 = min(c0 + GROUP_W, n)
        o_ref[0, :, c0:c1] = jnp.dot(h_scr[...], w_ref[:, c0:c1], preferred_element_type=f32)


def _in_projection(x, mod_l, gain, w_perm):
    B, T, D = x.shape
    tm = TOKEN_TILE
    return pl.pallas_call(
        _inproj_kernel,
        out_shape=jax.ShapeDtypeStruct((B, T, PROJ_W), f32),
        grid=(B, T // tm),
        in_specs=[
            pl.BlockSpec((1, tm, D), lambda b, t: (b, t, 0)),
            pl.BlockSpec((1, 6, D), lambda b, t: (b, 0, 0)),
            _const_spec((1, D)),
            _const_spec((D, PROJ_W)),
        ],
        out_specs=pl.BlockSpec((1, tm, PROJ_W), lambda b, t: (b, t, 0)),
        scratch_shapes=[pltpu.VMEM((tm, D), bf16)],
        compiler_params=_params(("arbitrary", "arbitrary")),
        name="in_projection",
    )(x, mod_l, gain.reshape(1, D), w_perm)


def _shift_rows(x, prev8, s):
    xs = pltpu.roll(x, s, axis=0)
    fix = pltpu.roll(prev8, s, axis=0)
    row = lax.broadcasted_iota(jnp.int32, fix.shape, 0)
    head = jnp.where(row < s, fix, xs[:8])
    return jnp.concatenate([head, xs[8:]], axis=0)


def _causal_conv(x, prev8, w_ref, taps):
    acc = x * w_ref[taps - 1:taps, :]
    for s in range(1, taps):
        acc = acc + _shift_rows(x, prev8, s) * w_ref[taps - 1 - s:taps - s, :]
    return acc


def _block_lower_inverses(lows, base, top, out):
    n = lows[0].shape[0]
    row = lax.broadcasted_iota(jnp.int32, (n, n), 0)
    col = lax.broadcasted_iota(jnp.int32, (n, n), 1)
    same_block = lambda size: (row // size) == (col // size)
    eye = jnp.where(row == col, 1.0, 0.0).astype(f32)
    base_mask = same_block(base)
    lbs = [jnp.where(base_mask, low, 0.0) for low in lows]
    xs = [eye - lb for lb in lbs]
    lbs = [lb.astype(bf16) for lb in lbs]
    ps = [jnp.dot(lb, lb, preferred_element_type=f32) for lb in lbs]
    yield
    for _ in range(int(math.log2(base)) - 2):
        pbs = [p.astype(bf16) for p in ps]
        xps = [jnp.dot(jnp.concatenate([x.astype(bf16), pb], axis=0), pb, preferred_element_type=f32)
               for x, pb in zip(xs, pbs)]
        yield
        xs = [x + xp[:n] for x, xp in zip(xs, xps)]
        ps = [xp[n:] for xp in xps]
    xs = [x + _bdot(x, p) for x, p in zip(xs, ps)]
    yield
    size = base
    while size < top:
        off_mask = jnp.logical_and(same_block(2 * size), jnp.logical_not(same_block(size)))
        offs = [jnp.where(off_mask, low, 0.0).astype(bf16) for low in lows]
        xbs = [x.astype(bf16) for x in xs]
        ts = [jnp.dot(xb, off, preferred_element_type=f32) for xb, off in zip(xbs, offs)]
        yield
        ts = [jnp.dot(t.astype(bf16), xb, preferred_element_type=f32) for t, xb in zip(ts, xbs)]
        yield
        xs = [x - t for x, t in zip(xs, ts)]
        size *= 2
    out.extend(xs)


def _gdn_kernel(qkv_ref, gate_ref, ba_ref, convw_ref, alog_ref, dt_ref, onorm_ref, o_ref,
                q_s, k_s, v_s, gc_s, beta_s, state_s, carry_s):
    tg = qkv_ref.shape[1]
    C = GDN_CHUNK

    @pl.when(pl.program_id(1) == 0)
    def _():
        state_s[...] = jnp.zeros_like(state_s)
        carry_s[...] = jnp.zeros_like(carry_s)

    def token_prep(r0, nr):
        rows = slice(r0, r0 + nr)
        for grp, dst in enumerate((q_s, k_s, v_s)):
            for h in range(GDN_HEADS):
                c0 = grp * GROUP_W + h * HEAD_W
                cols = slice(c0, c0 + HEAD_W)
                prev8 = carry_s[:, cols] if r0 == 0 else qkv_ref[0, r0 - 8:r0, cols]
                y = _silu(_causal_conv(qkv_ref[0, rows, cols], prev8, convw_ref.at[:, cols], GDN_CONV))
                if grp < 2:
                    y = y * lax.rsqrt(jnp.sum(y * y, axis=-1, keepdims=True) + EPS)
                if grp == 0:
                    y = y * (GDN_DK ** -0.5)
                dst[rows, h * HEAD_W:(h + 1) * HEAD_W] = y
        ba = ba_ref[0, rows, :]
        beta_s[rows, :] = 1.0 / (1.0 + jnp.exp(-ba))
        z = ba + dt_ref[...]
        g = -jnp.exp(alog_ref[...]) * (jnp.maximum(z, 0.0) + jnp.log1p(jnp.exp(-jnp.abs(z))))
        rowc = lax.broadcasted_iota(jnp.int32, g.shape, 0) % C
        s = 1
        while s < C:
            g = g + jnp.where(rowc >= s, pltpu.roll(g, s, axis=0), 0.0)
            s *= 2
        gc_s[rows, :] = g

    H = GDN_HEADS
    n = H * C
    ri = lax.broadcasted_iota(jnp.int32, (n, n), 0)
    ci = lax.broadcasted_iota(jnp.int32, (n, n), 1)
    same_head = (ri // C) == (ci // C)
    incl = jnp.logical_and(same_head, ri >= ci)
    strict = jnp.logical_and(same_head, ri > ci)
    onorm = onorm_ref[...]

    def chunk_operands(c):
        rows = slice(c * C, (c + 1) * C)
        gcc = gc_s[rows, :]
        gct = gcc.T
        betac = beta_s[rows, :]
        glast = gc_s[(c + 1) * C - 1:(c + 1) * C, :]
        ks, kbs, qs, rhs, qgs, kdts, gcols, grows, egls = [], [], [], [], [], [], [], [], []
        for h in range(H):
            cols = slice(h * HEAD_W, (h + 1) * HEAD_W)
            q = q_s[rows, cols]
            k = k_s[rows, cols]
            v = v_s[rows, cols]
            beta = betac[:, h:h + 1]
            gcol = gcc[:, H + h:H + h + 1]
            gl = glast[:, H + h:H + h + 1]
            eg = jnp.exp(gcol)
            kb = k * beta
            ks.append(k)
            kbs.append(kb)
            qs.append(q)
            rhs.append(jnp.concatenate([v * beta, kb * eg], axis=1))
            qgs.append((q * eg).astype(bf16))
            kdts.append((k * jnp.exp(gl - gcol)).T.astype(bf16))
            gcols.append(gcol)
            grows.append(gct[H + h:H + h + 1, :])
            egls.append(jnp.exp(gl))
        gcol_all = jnp.concatenate(gcols, axis=0)
        grow_all = jnp.concatenate(grows, axis=1)
        decay = jnp.exp(jnp.where(incl, gcol_all - grow_all, NEG_BIG))
        return dict(kbq=jnp.concatenate(kbs + qs, axis=0).astype(bf16), k_all=jnp.concatenate(ks, axis=0).astype(bf16),
                    rhs=jnp.concatenate(rhs, axis=0).astype(bf16), decay=decay, qg=qgs, kdt=kdts, egl=egls)

    def state_free_part(c0):
        ops = [chunk_operands(c) for c in range(c0, c0 + GDN_LOCKSTEP)]
        if c0 + GDN_LOCKSTEP < tg // C:
            token_prep((c0 + GDN_LOCKSTEP) * C, GDN_LOCKSTEP * C)
        kqs = [lax.dot_general(o["kbq"], o["k_all"], (((1,), (1,)), ((), ())), preferred_element_type=f32)
               for o in ops]
        yield
        lows = [jnp.where(strict, kq[:n] * o["decay"], 0.0) for kq, o in zip(kqs, ops)]
        for kq, o in zip(kqs, ops):
            o["a"] = (kq[n:] * o["decay"]).astype(bf16)
        t_invs = []
        yield from _block_lower_inverses(lows, 16, C, t_invs)
        for t_inv, o in zip(t_invs, ops):
            o["uw"] = jnp.dot(t_inv.astype(bf16), o["rhs"], preferred_element_type=f32)
        yield
        chunks.extend(ops)

    def recurrence(c_first, c_last):
        for c in range(c_first, c_last):
            o = chunks[c]
            rows = slice(c * C, (c + 1) * C)
            uw = o["uw"]
            wss = [jnp.dot(jnp.concatenate([uw[h * C:(h + 1) * C, HEAD_W:].astype(bf16), o["qg"][h]], axis=0),
                           states[h].astype(bf16), preferred_element_type=f32) for h in range(H)]
            yield
            v_news = [uw[h * C:(h + 1) * C, :HEAD_W] - wss[h][:C] for h in range(H)]
            for h in range(H):
                states[h] = (states[h] * o["egl"][h]
                             + jnp.dot(o["kdt"][h], v_news[h].astype(bf16), preferred_element_type=f32))
            yield
            o_all = (jnp.concatenate([ws[C:] for ws in wss], axis=0)
                     + jnp.dot(o["a"], jnp.concatenate(v_news, axis=0).astype(bf16), preferred_element_type=f32))
            yield
            for h in range(H):
                cols = slice(h * HEAD_W, (h + 1) * HEAD_W)
                oh = o_all[h * C:(h + 1) * C]
                on = oh * lax.rsqrt(jnp.mean(oh * oh, axis=-1, keepdims=True) + EPS) * onorm
                o_ref[0, rows, cols] = on * _silu(gate_ref[0, rows, cols])

    chunks = []
    states = [state_s[h] for h in range(H)]
    token_prep(0, GDN_LOCKSTEP * C)
    pending = iter(())
    for c0 in range(0, tg // C, GDN_LOCKSTEP):
        for _ in state_free_part(c0):
            next(pending, None)
        for _ in pending:
            pass
        pending = recurrence(c0, c0 + GDN_LOCKSTEP)
    for _ in pending:
        pass
    for h in range(H):
        state_s[h] = states[h]
    carry_s[...] = qkv_ref[0, tg - 8:tg, :]


def _gdn_mixer(proj, conv_w, a_log, dt_bias, out_norm):
    B, T, _ = proj.shape
    tg = GDN_TOKEN_TILE
    convw =jnp.zeros((8, 3 * GROUP_W), f32).at[:GDN_CONV].set(conv_w.astype(f32))
    lane_pad = lambda v: jnp.zeros((1, HEAD_W), f32).at[0, GDN_HEADS:2 * GDN_HEADS].set(v.astype(f32))
    return pl.pallas_call(
        _gdn_kernel,
        out_shape=jax.ShapeDtypeStruct((B, T, GROUP_W), f32),
        grid=(B, T // tg),
        in_specs=[
            pl.BlockSpec((1, tg, 3 * GROUP_W), lambda b, t: (b, t, 0)),
            pl.BlockSpec((1, tg, GROUP_W), lambda b, t: (b, t, COL_GATE // GROUP_W)),
            pl.BlockSpec((1, tg, HEAD_W), lambda b, t: (b, t, COL_BA // HEAD_W)),
            _const_spec((8, 3 * GROUP_W)),
            _const_spec((1, HEAD_W)),
            _const_spec((1, HEAD_W)),
            _const_spec((1, HEAD_W)),
        ],
        out_specs=pl.BlockSpec((1, tg, GROUP_W), lambda b, t: (b, t, 0)),
        scratch_shapes=[
            pltpu.VMEM((tg, GROUP_W), f32),
            pltpu.VMEM((tg, GROUP_W), f32),
            pltpu.VMEM((tg, GROUP_W), f32),
            pltpu.VMEM((tg, HEAD_W), f32),
            pltpu.VMEM((tg, HEAD_W), f32),
            pltpu.VMEM((GDN_HEADS, GDN_DK, HEAD_W), f32),
            pltpu.VMEM((8, 3 * GROUP_W), f32),
        ],
        compiler_params=_params(("arbitrary", "arbitrary")),
        name="gdn_mixer",
    )(proj, proj, proj, convw, lane_pad(a_log), lane_pad(dt_bias), out_norm.reshape(1, HEAD_W).astype(f32))


def _half_rmsnorm(x, gain_row):
    w = x.shape[1]
    ri = lax.broadcasted_iota(jnp.int32, (w, w), 0)
    ci = lax.broadcasted_iota(jnp.int32, (w, w), 1)
    same_half = jnp.where((ri // DIFF_DH) == (ci // DIFF_DH), 1.0, 0.0).astype(bf16)
    x2 = x * x
    hi = x2.astype(bf16)
    lo = (x2 - hi.astype(f32)).astype(bf16)
    ss = (jnp.dot(hi, same_half, preferred_element_type=f32)
          + jnp.dot(lo, same_half, preferred_element_type=f32))
    return x * lax.rsqrt(ss * (1.0 / DIFF_DH) + EPS) * gain_row


def _attn_kernel(q_ref, k_ref, v_ref, db_ref, qn_ref, kn_ref, lam_ref, subln_ref, o_ref,
                 k_s, vt_s, q_s, m_s, acc_s, *, lambda_init):
    tq = q_ref.shape[1]
    tk = db_ref.shape[2]
    T = k_ref.shape[1]
    dv = v_ref.shape[2]
    qi = pl.program_id(2)

    @pl.when(qi == 0)
    def _():
        ones_rows = jnp.where(lax.broadcasted_iota(jnp.int32, (vt_s.shape[1] - dv, tk), 0) == 0, 1.0, 0.0)

        def prep(i, carry):
            tiles = [i * ATTN_PREP_TILES + u for u in range(ATTN_PREP_TILES)]
            rows = [pl.ds(pl.multiple_of(j * tk, tk), tk) for j in tiles]
            kn = [_half_rmsnorm(k_ref[0, r, :], kn_ref[...]) for r in rows]
            vt = [v_ref[0, r, :].T for r in rows]
            for j, k, v in zip(tiles, kn, vt):
                k_s[j] = k.astype(bf16)
                vt_s[j] = jnp.concatenate([v, ones_rows], axis=0).astype(bf16)
            return carry
        lax.fori_loop(0, T // (tk * ATTN_PREP_TILES), prep, 0)

    qt = (_half_rmsnorm(q_ref[0], qn_ref[...]) * (DIFF_DH ** -0.5 * LOG2E)).T
    dim = lax.broadcasted_iota(jnp.int32, qt.shape, 0)
    q_s[0] = jnp.where(dim < DIFF_DH, qt, 0.0).astype(bf16)
    q_s[1] = jnp.where(dim < DIFF_DH, 0.0, qt).astype(bf16)

    r = tq // tk
    n_near = db_ref.shape[0]
    near_tiles = [r * (qi + 1) - 1 - t for t in range(n_near)]
    near_bias = [db_ref[t, 0] for t in range(n_near)]
    n_far = jnp.maximum(r * qi - 1, 0)
    n_grouped = (n_far // ATTN_GROUP) * ATTN_GROUP

    def unshifted(js, biases, acc):
        units = [(t, c) for t in range(len(js)) for c in range(2)]
        scores = {}
        for i in range(len(units) + ATTN_LOOKAHEAD):
            if i < len(units):
                t, c = units[i]
                s = jnp.dot(k_s[js[t]], q_s[c], preferred_element_type=f32)
                scores[i] = s if biases[t] is None else s + biases[t][:, c * tq:(c + 1) * tq]
            if i >= ATTN_LOOKAHEAD:
                t, c = units[i - ATTN_LOOKAHEAD]
                pv = jnp.dot(vt_s[js[t]], jnp.exp2(scores.pop(i - ATTN_LOOKAHEAD)).astype(bf16),
                             preferred_element_type=f32)
                acc[c] = pv if acc[c] is None else acc[c] + pv
        return acc

    @pl.when(qi == 0)
    def _():
        acc = unshifted(near_tiles[:r], near_bias[:r], [None, None])
        for c in range(2):
            acc_s[c] = acc[c]

    for rem in range(ATTN_GROUP):
        @pl.when(jnp.logical_and(qi >= 1, n_far - n_grouped == rem))
        def _():
            acc = unshifted(near_tiles + [n_grouped + t for t in range(rem)],
                            near_bias + [None] * rem, [None, None])
            for c in range(2):
                acc_s[c] = acc[c]

    def far_group(i, carry):
        acc = unshifted([ATTN_GROUP * i + t for t in range(ATTN_GROUP)], [None] * ATTN_GROUP,
                        [acc_s[0], acc_s[1]])
        for c in range(2):
            acc_s[c] = acc[c]
        return carry
    lax.fori_loop(0, n_far // ATTN_GROUP, far_group, 0)

    denom = jnp.concatenate([acc_s[0, dv:dv + 1, :], acc_s[1, dv:dv + 1, :]], axis=1)
    trusted = jnp.logical_and(jnp.min(denom) >= SOFTMAX_DENOM_MIN, jnp.max(denom) <= SOFTMAX_DENOM_MAX)

    @pl.when(jnp.logical_not(trusted))
    def _():
        m_s[...] = jnp.full_like(m_s, NEG_BIG)
        acc_s[...] = jnp.zeros_like(acc_s)

        def online(js, biases):
            for c in range(2):
                ss = []
                for j, bias in zip(js, biases):
                    s = jnp.dot(k_s[j], q_s[c], preferred_element_type=f32)
                    ss.append(s if bias is None else s + bias[:, c * tq:(c + 1) * tq])
                m_old = m_s[c]
                m_new = m_old
                for s in ss:
                    m_new = jnp.maximum(m_new, jnp.max(s, axis=0, keepdims=True))
                acc = jnp.exp2(m_old - m_new) * acc_s[c]
                for j, s in zip(js, ss):
                    acc = acc + jnp.dot(vt_s[j], jnp.exp2(s - m_new).astype(bf16), preferred_element_type=f32)
                acc_s[c] = acc
                m_s[c] = m_new

        for t in range(r):
            online([near_tiles[t]], [near_bias[t]])

        @pl.when(qi >= 1)
        def _():
            online([near_tiles[r]], [near_bias[r]])

        def far_tile(j, carry):
            online([j], [None])
            return carry
        lax.fori_loop(0, n_far, far_tile, 0)

    lam_rows = lam_ref[...]
    lam = (jnp.exp(jnp.sum(lam_rows[0:1] * lam_rows[1:2], axis=-1, keepdims=True))
           - jnp.exp(jnp.sum(lam_rows[2:3] * lam_rows[3:4], axis=-1, keepdims=True)) + lambda_init)
    acc1 = acc_s[0]
    acc2 = acc_s[1]
    ot = acc1[:dv] * (1.0 / acc1[dv:dv + 1]) - lam * (acc2[:dv] * (1.0 / acc2[dv:dv + 1]))
    ont = ot * lax.rsqrt(jnp.mean(ot * ot, axis=0, keepdims=True) + EPS)
    o_ref[0] = ont.T * (subln_ref[...] * (1.0 - lambda_init))


def _diff_attention(proj, db, q_norm, k_norm, lq1, lk1, lq2, lk2, subln, lambda_init):
    B, T, _ = proj.shape
    tq, tk = ATTN_TQ, ATTN_TK
    two = lambda v: jnp.concatenate([v, v]).reshape(1, HEAD_W).astype(f32)
    lam_rows = jnp.zeros((8, DIFF_DH), f32).at[:4].set(jnp.stack([lq1, lk1, lq2, lk2]).astype(f32))
    qb, kb, vb = COL_DQ // HEAD_W, COL_DK // HEAD_W, COL_DV // HEAD_W
    return pl.pallas_call(
        functools.partial(_attn_kernel, lambda_init=lambda_init),
        out_shape=jax.ShapeDtypeStruct((B, T, GROUP_W), f32),
        grid=(B, DIFF_HEADS, T // tq),
        in_specs=[
            pl.BlockSpec((1, tq, HEAD_W), lambda b, h, i: (b, i, qb + h)),
            pl.BlockSpec((1, T, HEAD_W), lambda b, h, i: (b, 0, kb + h)),
            pl.BlockSpec((1, T, HEAD_W), lambda b, h, i: (b, 0, vb + h)),
            pl.BlockSpec((tq // tk + 1, 1, tk, 2 * tq), lambda b, h, i: (0, h, 0, 0)),
            _const_spec((1, HEAD_W)),
            _const_spec((1, HEAD_W)),
            _const_spec((8, DIFF_DH)),
            _const_spec((1, HEAD_W)),
        ],
        out_specs=pl.BlockSpec((1, tq, HEAD_W), lambda b, h, i: (b, i, h)),
        scratch_shapes=[
            pltpu.VMEM((T // tk, tk, HEAD_W), bf16),
            pltpu.VMEM((T // tk, HEAD_W + BF16_ROWS, tk), bf16),
            pltpu.VMEM((2, HEAD_W, tq), bf16),
            pltpu.VMEM((2, 1, tq), f32),
            pltpu.VMEM((2, HEAD_W + BF16_ROWS, tq), f32),
        ],
        compiler_params=_params(("arbitrary", "arbitrary", "arbitrary")),
        name="diff_attention",
    )(proj, proj, proj, db, two(q_norm), two(k_norm), lam_rows, subln.reshape(1, HEAD_W).astype(f32))


def _outffn_kernel(x_ref, yg_ref, yd_ref, mod_ref, g_ref, wout_ref, wup_ref, cw_ref, wdown_ref, o_ref,
                   h_s, acc_s, carry_s):
    nchunks = wup_ref.shape[0]
    fc = wdown_ref.shape[1]

    @pl.when(pl.program_id(1) == 0)
    def _():
        carry_s[...] = jnp.zeros_like(carry_s)

    half = yg_ref.shape[2]
    y = (jnp.dot(yg_ref[0].astype(bf16), wout_ref[:half, :], preferred_element_type=f32)
         + jnp.dot(yd_ref[0].astype(bf16), wout_ref[half:, :], preferred_element_type=f32))
    x1 = x_ref[0] + mod_ref[0, 2:3, :] * y
    o_ref[0] = x1
    h_s[...] = _modulated_norm(x1, g_ref[...], mod_ref, 3, 4).astype(bf16)
    acc_s[...] = jnp.zeros_like(acc_s)
    tm = h_s.shape[0]

    tr = tm // FFN_ROW_SPLIT

    def up(j):
        return [jnp.dot(h_s[r0:r0 + tr, :], wup_ref[j], preferred_element_type=f32)
                for r0 in range(0, tm, tr)]

    us = up(0)
    for j in range(nchunks):
        us_next = up(j + 1) if j + 1 < nchunks else None
        cw = cw_ref.at[j]
        prev8 = carry_s[j]
        for i, u in enumerate(us):
            yc = _causal_conv(u, prev8, cw, FFN_CONV) + cw[FFN_CONV:FFN_CONV + 1, :]
            prev8 = u[tr - 8:, :]
            act = _silu(yc[:, fc:]) * yc[:, :fc]
            acc_s[i * tr:(i + 1) * tr, :] += jnp.dot(act.astype(bf16), wdown_ref[j], preferred_element_type=f32)
        carry_s[j] = prev8
        us = us_next
    o_ref[0] = o_ref[0] + mod_ref[0, 5:6, :] * acc_s[...]


def _out_ffn(x, y_gdn, y_diff, mod_l, gain, w_out, w_up_c, conv_c, w_down_c):
    B, T, D = x.shape
    tm = TOKEN_TILE
    nchunks, _, fc2 = w_up_c.shape
    fc = fc2 // 2
    return pl.pallas_call(
        _outffn_kernel,
        out_shape=jax.ShapeDtypeStruct((B, T, D), f32),
        grid=(B, T // tm),
        in_specs=[
            pl.BlockSpec((1, tm, D), lambda b, t: (b, t, 0)),
            pl.BlockSpec((1, tm, GROUP_W), lambda b, t: (b, t, 0)),
            pl.BlockSpec((1, tm, GROUP_W), lambda b, t: (b, t, 0)),
            pl.BlockSpec((1, 6, D), lambda b, t: (b, 0, 0)),
            _const_spec((1, D)),
            _const_spec((2 * GROUP_W, D)),
            _const_spec((nchunks, D, fc2)),
            _const_spec((nchunks, 8, fc2)),
            _const_spec((nchunks, fc, D)),
        ],
        out_specs=pl.BlockSpec((1, tm, D), lambda b, t: (b, t, 0)),
        scratch_shapes=[
            pltpu.VMEM((tm, D), bf16),
            pltpu.VMEM((tm, D), f32),
            pltpu.VMEM((nchunks, 8, fc2), f32),
        ],
        compiler_params=_params(("arbitrary", "arbitrary")),
        name="out_ffn",
    )(x, y_gdn, y_diff, mod_l, gain.reshape(1, D), w_out, w_up_c, conv_c, w_down_c)


def _regroup_w_in(w_in_l):
    D = w_in_l.shape[0]
    o_b = 4 * GROUP_W
    o_d = o_b + 2 * GDN_HEADS
    small = jnp.zeros((D, HEAD_W), w_in_l.dtype).at[:, :2 * GDN_HEADS].set(w_in_l[:, o_b:o_d])
    return jnp.concatenate([w_in_l[:, :o_b], w_in_l[:, o_d:], small], axis=1).astype(bf16)


def _chunk_ffn(ffn_up_l, conv_w_l, conv_b_l, ffn_down_l, fc):
    D, two_f = ffn_up_l.shape
    F = two_f // 2
    n = F // fc
    pair = lambda m: jnp.concatenate([m[..., :F].reshape(m.shape[:-1] + (n, fc)),
                                      m[..., F:].reshape(m.shape[:-1] + (n, fc))], axis=-1)
    w_up_c = jnp.moveaxis(pair(ffn_up_l), -2, 0).astype(bf16)
    conv_rows = jnp.concatenate([conv_w_l.astype(f32), conv_b_l.astype(f32)[None]], axis=0)
    conv_c = jnp.moveaxis(pair(conv_rows), -2, 0)
    conv_c = jnp.zeros((n, 8, 2 * fc), f32).at[:, :FFN_CONV + 1].set(conv_c)
    w_down_c = ffn_down_l.reshape(n, fc, ffn_down_l.shape[1]).astype(bf16)
    return w_up_c, conv_c, w_down_c


def kernel(x, c, w_ada, b_ada, norm_mix, norm_ffn, w_in, gdn_conv_w, gdn_a_log, gdn_dt_bias, gdn_out_norm,
           diff_q_norm, diff_k_norm, diff_lambda_q1, diff_lambda_k1, diff_lambda_q2, diff_lambda_k2, diff_subln,
           rel_bias, w_out, ffn_up, ffn_conv_w, ffn_conv_b, ffn_down):
    B, T, D = x.shape
    depth = w_in.shape[0]
    mod = _ada_modulation(c, w_ada, b_ada).reshape(depth, B, 6, D)
    db = _bias_tiles(rel_bias, ATTN_TQ, ATTN_TK)
    for l in range(depth):
        proj = _in_projection(x, mod[l], norm_mix[l], _regroup_w_in(w_in[l]))
        y_gdn = _gdn_mixer(proj, gdn_conv_w[l], gdn_a_log[l], gdn_dt_bias[l], gdn_out_norm[l])
        lambda_init = 0.8 - 0.6 * math.exp(-0.3 * l)
        y_diff = _diff_attention(proj, db, diff_q_norm[l], diff_k_norm[l], diff_lambda_q1[l], diff_lambda_k1[l],
                                 diff_lambda_q2[l], diff_lambda_k2[l], diff_subln[l], lambda_init)
        w_up_c, conv_c, w_down_c = _chunk_ffn(ffn_up[l], ffn_conv_w[l], ffn_conv_b[l], ffn_down[l], FFN_CHUNK)
        x = _out_ffn(x, y_gdn, y_diff, mod[l], norm_ffn[l], w_out[l].astype(bf16), w_up_c, conv_c, w_down_c)
    return x
```

```python
import functools
import math

import numpy as np
import jax
import jax.numpy as jnp
from jax import lax
from jax.experimental import pallas as pl
from jax.experimental.pallas import tpu as pltpu

f32 = jnp.float32
bf16 = jnp.bfloat16

EPS = 1e-6
GDN_HEADS = 4
GDN_DK = 128
GDN_CONV = 4
GDN_CHUNK = 64
GDN_LOCKSTEP = 4
DIFF_HEADS = 4
DIFF_DH = 64
REL_BUCKETS = 32
REL_MAX_DIST = 128
FFN_CONV = 3
HEAD_W = 128
GROUP_W = GDN_HEADS * HEAD_W
NEG_BIG = -1e30
LOG2E = math.log2(math.e)
BF16_ROWS = 16

VMEM_LIMIT_BYTES = 56 * 1024 * 1024

COL_GQ, COL_GK, COL_GV, COL_GATE, COL_DQ, COL_DK, COL_DV, COL_BA = (
    0, GROUP_W, 2 * GROUP_W, 3 * GROUP_W, 4 * GROUP_W, 5 * GROUP_W, 6 * GROUP_W, 7 * GROUP_W)
PROJ_W = 7 * GROUP_W + HEAD_W

TOKEN_TILE = 512
GDN_TOKEN_TILE = 1024
ATTN_TQ = 512
ATTN_TK = 256
ATTN_PREP_TILES = 4
ATTN_GROUP = 8
ATTN_LOOKAHEAD = 4
SOFTMAX_DENOM_MIN = 2.0 ** -64
SOFTMAX_DENOM_MAX = 2.0 ** 80
FFN_CHUNK = 256
FFN_DOWN_GROUP = 11
FFN_ROW_SPLIT = 2


def _bdot(a, b):
    return jnp.dot(a.astype(bf16), b.astype(bf16), preferred_element_type=f32)


def _bdot_nt(a, b):
    return lax.dot_general(a.astype(bf16), b.astype(bf16), (((1,), (1,)), ((), ())),
                           preferred_element_type=f32)


def _silu(x):
    return x * (1.0 / (1.0 + jnp.exp(-x)))


def _params(semantics):
    return pltpu.CompilerParams(dimension_semantics=semantics, vmem_limit_bytes=VMEM_LIMIT_BYTES)


def _const_spec(shape):
    nd = len(shape)
    return pl.BlockSpec(shape, lambda *_: (0,) * nd, pipeline_mode=pl.Buffered(1))


def _ada_kernel(c_ref, w_ref, b_ref, o_ref):
    ca = _silu(c_ref[...])
    o_ref[0] = _bdot(ca, w_ref[0]) + b_ref[0]


def _ada_modulation(c, w_ada, b_ada):
    L, D, N = w_ada.shape
    B = c.shape[0]
    tn = 1536
    return pl.pallas_call(
        _ada_kernel,
        out_shape=jax.ShapeDtypeStruct((L, B, N), f32),
        grid=(L, N // tn),
        in_specs=[
            pl.BlockSpec((B, D), lambda l, n: (0, 0)),
            pl.BlockSpec((1, D, tn), lambda l, n: (l, 0, n)),
            pl.BlockSpec((1, 1, tn), lambda l, n: (l, 0, n)),
        ],
        out_specs=pl.BlockSpec((1, B, tn), lambda l, n: (l, 0, n)),
        compiler_params=_params(("arbitrary", "arbitrary")),
        name="ada_modulation",
    )(c, w_ada, b_ada.reshape(L, 1, N))


def _bucket_of_distance(n):
    max_exact = REL_BUCKETS // 2
    nf = np.maximum(n, 1).astype(np.float32)
    large = max_exact + (np.log(nf / np.float32(max_exact)) / np.float32(math.log(REL_MAX_DIST / max_exact))
                         * np.float32(REL_BUCKETS - max_exact)).astype(np.int32)
    large = np.minimum(large, REL_BUCKETS - 1)
    return np.where(n < max_exact, n, large).astype(np.int32)


def _bias_bucket_tiles(tq, tk):
    assert tq % tk == 0
    r = tq // tk
    i = np.arange(tq)[None, :]
    j = np.arange(tk)[:, None]
    tiles = []
    for t in range(r + 1):
        dist = i - j + (t - (r - 1)) * tk
        tiles.append(np.where(dist >= 0, _bucket_of_distance(np.maximum(dist, 0)), -1))
    assert _bucket_of_distance(np.array([tk + 1]))[0] == REL_BUCKETS - 1
    near = np.stack(tiles).astype(np.int32)
    return np.concatenate([near, near], axis=2)


def _bias_tile_kernel(rel_ref, bk_ref, o_ref):
    h = pl.program_id(1)
    bk = bk_ref[0]
    last = rel_ref[REL_BUCKETS - 1, h]
    acc = jnp.where(bk < 0, NEG_BIG, 0.0).astype(f32)
    for b in range(REL_BUCKETS - 1):
        acc = jnp.where(bk == b, (rel_ref[b, h] - last) * LOG2E, acc)
    o_ref[0, 0] = acc


def _bias_tiles(rel_bias, tq, tk):
    buckets = jnp.asarray(_bias_bucket_tiles(tq, tk))
    return pl.pallas_call(
        _bias_tile_kernel,
        out_shape=jax.ShapeDtypeStruct((buckets.shape[0], DIFF_HEADS, tk, 2 * tq), f32),
        grid=(buckets.shape[0], DIFF_HEADS),
        in_specs=[
            pl.BlockSpec(memory_space=pltpu.SMEM),
            pl.BlockSpec((1, tk, 2 * tq), lambda k, h: (k, 0, 0)),
        ],
        out_specs=pl.BlockSpec((1, 1, tk, 2 * tq), lambda k, h: (k, h, 0, 0)),
        compiler_params=_params(("arbitrary", "arbitrary")),
        name="rel_bias_tiles",
    )(rel_bias.astype(f32), buckets)


def _modulated_norm(x, gain_row, mod_ref, shift_idx, scale_idx):
    ms = jnp.mean(x * x, axis=-1, keepdims=True)
    y = x * lax.rsqrt(ms + EPS)
    return y * (gain_row * (1.0 + mod_ref[0, scale_idx:scale_idx + 1, :])) + mod_ref[0, shift_idx:shift_idx + 1, :]


def _inproj_kernel(x_ref, mod_ref, g_ref, w_ref, o_ref, h_scr):
    h_scr[...] = _modulated_norm(x_ref[0], g_ref[...], mod_ref, 0, 1).astype(bf16)
    n = o_ref.shape[2]
    for c0 in range(0, n, GROUP_W):
        c1 = min(c0 + GROUP_W, n)
        o_ref[0, :, c0:c1] = jnp.dot(h_scr[...], w_ref[:, c0:c1], preferred_element_type=f32)


def _in_projection(x, mod_l, gain, w_perm):
    B, T, D = x.shape
    tm = TOKEN_TILE
    return pl.pallas_call(
        _inproj_kernel,
        out_shape=jax.ShapeDtypeStruct((B, T, PROJ_W), f32),
        grid=(B, T // tm),
        in_specs=[
            pl.BlockSpec((1, tm, D), lambda b, t: (b, t, 0)),
            pl.BlockSpec((1, 6, D), lambda b, t: (b, 0, 0)),
            _const_spec((1, D)),
            _const_spec((D, PROJ_W)),
        ],
        out_specs=pl.BlockSpec((1, tm, PROJ_W), lambda b, t: (b, t, 0)),
        scratch_shapes=[pltpu.VMEM((tm, D), bf16)],
        compiler_params=_params(("arbitrary", "arbitrary")),
        name="in_projection",
    )(x, mod_l, gain.reshape(1, D), w_perm)


def _shift_rows(x, prev8, s):
    xs = pltpu.roll(x, s, axis=0)
    fix = pltpu.roll(prev8, s, axis=0)
    row = lax.broadcasted_iota(jnp.int32, fix.shape, 0)
    head = jnp.where(row < s, fix, xs[:8])
    return jnp.concatenate([head, xs[8:]], axis=0)


def _causal_conv(x, prev8, w_ref, taps):
    acc = x * w_ref[taps - 1:taps, :]
    for s in range(1, taps):
        acc = acc + _shift_rows(x, prev8, s) * w_ref[taps - 1 - s:taps - s, :]
    return acc


def _block_lower_inverses(lows, base, top, out):
    n = lows[0].shape[0]
    row = lax.broadcasted_iota(jnp.int32, (n, n), 0)
    col = lax.broadcasted_iota(jnp.int32, (n, n), 1)
    same_block = lambda size: (row // size) == (col // size)
    eye = jnp.where(row == col, 1.0, 0.0).astype(f32)
    base_mask = same_block(base)
    lbs = [jnp.where(base_mask, low, 0.0) for low in lows]
    xs = [eye - lb for lb in lbs]
    lbs = [lb.astype(bf16) for lb in lbs]
    ps = [jnp.dot(lb, lb, preferred_element_type=f32) for lb in lbs]
    yield
    for _ in range(int(math.log2(base)) - 2):
        pbs = [p.astype(bf16) for p in ps]
        xps = [jnp.dot(jnp.concatenate([x.astype(bf16), pb], axis=0), pb, preferred_element_type=f32)
               for x, pb in zip(xs, pbs)]
        yield
        xs = [x + xp[:n] for x, xp in zip(xs, xps)]
        ps = [xp[n:] for xp in xps]
    xs = [x + _bdot(x, p) for x, p in zip(xs, ps)]
    yield
    size = base
    while size < top:
        off_mask = jnp.logical_and(same_block(2 * size), jnp.logical_not(same_block(size)))
        offs = [jnp.where(off_mask, low, 0.0).astype(bf16) for low in lows]
        xbs = [x.astype(bf16) for x in xs]
        ts = [jnp.dot(xb, off, preferred_element_type=f32) for xb, off in zip(xbs, offs)]
        yield
        ts = [jnp.dot(t.astype(bf16), xb, preferred_element_type=f32) for t, xb in zip(ts, xbs)]
        yield
        xs = [x - t for x, t in zip(xs, ts)]
        size *= 2
    out.extend(xs)


def _gdn_kernel(qkv_ref, gate_ref, ba_ref, convw_ref, alog_ref, dt_ref, onorm_ref, o_ref,
                q_s, k_s, v_s, gc_s, beta_s, state_s, carry_s):
    tg = qkv_ref.shape[1]
    C = GDN_CHUNK

    @pl.when(pl.program_id(1) == 0)
    def _():
        state_s[...] = jnp.zeros_like(state_s)
        carry_s[...] = jnp.zeros_like(carry_s)

    def token_prep(r0, nr):
        rows = slice(r0, r0 + nr)
        for grp, dst in enumerate((q_s, k_s, v_s)):
            for h in range(GDN_HEADS):
                c0 = grp * GROUP_W + h * HEAD_W
                cols = slice(c0, c0 + HEAD_W)
                prev8 = carry_s[:, cols] if r0 == 0 else qkv_ref[0, r0 - 8:r0, cols]
                y = _silu(_causal_conv(qkv_ref[0, rows, cols], prev8, convw_ref.at[:, cols], GDN_CONV))
                if grp < 2:
                    y = y * lax.rsqrt(jnp.sum(y * y, axis=-1, keepdims=True) + EPS)
                if grp == 0:
                    y = y * (GDN_DK ** -0.5)
                dst[rows, h * HEAD_W:(h + 1) * HEAD_W] = y
        ba = ba_ref[0, rows, :]
        beta_s[rows, :] = 1.0 / (1.0 + jnp.exp(-ba))
        z = ba + dt_ref[...]
        g = -jnp.exp(alog_ref[...]) * (jnp.maximum(z, 0.0) + jnp.log1p(jnp.exp(-jnp.abs(z))))
        rowc = lax.broadcasted_iota(jnp.int32, g.shape, 0) % C
        s = 1
        while s < C:
            g = g + jnp.where(rowc >= s, pltpu.roll(g, s, axis=0), 0.0)
            s *= 2
        gc_s[rows, :] = g

    H = GDN_HEADS
    n = H * C
    ri = lax.broadcasted_iota(jnp.int32, (n, n), 0)
    ci = lax.broadcasted_iota(jnp.int32, (n, n), 1)
    same_head = (ri // C) == (ci // C)
    incl = jnp.logical_and(same_head, ri >= ci)
    strict = jnp.logical_and(same_head, ri > ci)
    onorm = onorm_ref[...]

    def chunk_operands(c):
        rows = slice(c * C, (c + 1) * C)
        gcc = gc_s[rows, :]
        gct = gcc.T
        betac = beta_s[rows, :]
        glast = gc_s[(c + 1) * C - 1:(c + 1) * C, :]
        ks, kbs, qs, rhs, qgs, kdts, gcols, grows, egls = [], [], [], [], [], [], [], [], []
        for h in range(H):
            cols = slice(h * HEAD_W, (h + 1) * HEAD_W)
            q = q_s[rows, cols]
            k = k_s[rows, cols]
            v = v_s[rows, cols]
            beta = betac[:, h:h + 1]
            gcol = gcc[:, H + h:H + h + 1]
            gl = glast[:, H + h:H + h + 1]
            eg = jnp.exp(gcol)
            kb = k * beta
            ks.append(k)
            kbs.append(kb)
            qs.append(q)
            rhs.append(jnp.concatenate([v * beta, kb * eg], axis=1))
            qgs.append((q * eg).astype(bf16))
            kdts.append((k * jnp.exp(gl - gcol)).T.astype(bf16))
            gcols.append(gcol)
            grows.append(gct[H + h:H + h + 1, :])
            egls.append(jnp.exp(gl))
        gcol_all = jnp.concatenate(gcols, axis=0)
        grow_all = jnp.concatenate(grows, axis=1)
        decay = jnp.exp(jnp.where(incl, gcol_all - grow_all, NEG_BIG))
        return dict(kbq=jnp.concatenate(kbs + qs, axis=0).astype(bf16), k_all=jnp.concatenate(ks, axis=0).astype(bf16),
                    rhs=jnp.concatenate(rhs, axis=0).astype(bf16), decay=decay, qg=qgs, kdt=kdts, egl=egls)

    def state_free_part(c0):
        ops = [chunk_operands(c) for c in range(c0, c0 + GDN_LOCKSTEP)]
        if c0 + GDN_LOCKSTEP < tg // C:
            token_prep((c0 + GDN_LOCKSTEP) * C, GDN_LOCKSTEP * C)
        kqs = [lax.dot_general(o["kbq"], o["k_all"], (((1,), (1,)), ((), ())), preferred_element_type=f32)
               for o in ops]
        yield
        lows = [jnp.where(strict, kq[:n] * o["decay"], 0.0) for kq, o in zip(kqs, ops)]
        for kq, o in zip(kqs, ops):
            o["a"] = (kq[n:] * o["decay"]).astype(bf16)
        t_invs = []
        yield from _block_lower_inverses(lows, 16, C, t_invs)
        for t_inv, o in zip(t_invs, ops):
            o["uw"] = jnp.dot(t_inv.astype(bf16), o["rhs"], preferred_element_type=f32)
        yield
        chunks.extend(ops)

    def recurrence(c_first, c_last):
        for c in range(c_first, c_last):
            o = chunks[c]
            rows = slice(c * C, (c + 1) * C)
            uw = o["uw"]
            wss = [jnp.dot(jnp.concatenate([uw[h * C:(h + 1) * C, HEAD_W:].astype(bf16), o["qg"][h]], axis=0),
                           states[h].astype(bf16), preferred_element_type=f32) for h in range(H)]
            yield
            v_news = [uw[h * C:(h + 1) * C, :HEAD_W] - wss[h][:C] for h in range(H)]
            for h in range(H):
                states[h] = (states[h] * o["egl"][h]
                             + jnp.dot(o["kdt"][h], v_news[h].astype(bf16), preferred_element_type=f32))
            yield
            o_all = (jnp.concatenate([ws[C:] for ws in wss], axis=0)
                     + jnp.dot(o["a"], jnp.concatenate(v_news, axis=0).astype(bf16), preferred_element_type=f32))
            yield
            for h in range(H):
                cols = slice(h * HEAD_W, (h + 1) * HEAD_W)
                oh = o_all[h * C:(h + 1) * C]
                on = oh * lax.rsqrt(jnp.mean(oh * oh, axis=-1, keepdims=True) + EPS) * onorm
                o_ref[0, rows, cols] = (on * _silu(gate_ref[0, rows, cols])).astype(o_ref.dtype)

    chunks = []
    states = [state_s[h] for h in range(H)]
    token_prep(0, GDN_LOCKSTEP * C)
    pending = iter(())
    for c0 in range(0, tg // C, GDN_LOCKSTEP):
        for _ in state_free_part(c0):
            next(pending, None)
        for _ in pending:
            pass
        pending = recurrence(c0, c0 + GDN_LOCKSTEP)
    for _ in pending:
        pass
    for h in range(H):
        state_s[h] = states[h]
    carry_s[...] = qkv_ref[0, tg - 8:tg, :]


def _gdn_mixer(proj, conv_w, a_log, dt_bias, out_norm):
    B, T, _ = proj.shape
    tg = GDN_TOKEN_TILE
    convw =jnp.zeros((8, 3 * GROUP_W), f32).at[:GDN_CONV].set(conv_w.astype(f32))
    lane_pad = lambda v: jnp.zeros((1, HEAD_W), f32).at[0, GDN_HEADS:2 * GDN_HEADS].set(v.astype(f32))
    return pl.pallas_call(
        _gdn_kernel,
        out_shape=jax.ShapeDtypeStruct((B, T, GROUP_W), bf16),
        grid=(B, T // tg),
        in_specs=[
            pl.BlockSpec((1, tg, 3 * GROUP_W), lambda b, t: (b, t, 0)),
            pl.BlockSpec((1, tg, GROUP_W), lambda b, t: (b, t, COL_GATE // GROUP_W)),
            pl.BlockSpec((1, tg, HEAD_W), lambda b, t: (b, t, COL_BA // HEAD_W)),
            _const_spec((8, 3 * GROUP_W)),
            _const_spec((1, HEAD_W)),
            _const_spec((1, HEAD_W)),
            _const_spec((1, HEAD_W)),
        ],
        out_specs=pl.BlockSpec((1, tg, GROUP_W), lambda b, t: (b, t, 0)),
        scratch_shapes=[
            pltpu.VMEM((tg, GROUP_W), f32),
            pltpu.VMEM((tg, GROUP_W), f32),
            pltpu.VMEM((tg, GROUP_W), f32),
            pltpu.VMEM((tg, HEAD_W), f32),
            pltpu.VMEM((tg, HEAD_W), f32),
            pltpu.VMEM((GDN_HEADS, GDN_DK, HEAD_W), f32),
            pltpu.VMEM((8, 3 * GROUP_W), f32),
        ],
        compiler_params=_params(("arbitrary", "arbitrary")),
        name="gdn_mixer",
    )(proj, proj, proj, convw, lane_pad(a_log), lane_pad(dt_bias), out_norm.reshape(1, HEAD_W).astype(f32))


def _half_rmsnorm(x, gain_row):
    w = x.shape[1]
    ri = lax.broadcasted_iota(jnp.int32, (w, w), 0)
    ci = lax.broadcasted_iota(jnp.int32, (w, w), 1)
    same_half = jnp.where((ri // DIFF_DH) == (ci // DIFF_DH), 1.0, 0.0).astype(bf16)
    x2 = x * x
    hi = x2.astype(bf16)
    lo = (x2 - hi.astype(f32)).astype(bf16)
    ss = (jnp.dot(hi, same_half, preferred_element_type=f32)
          + jnp.dot(lo, same_half, preferred_element_type=f32))
    return x * lax.rsqrt(ss * (1.0 / DIFF_DH) + EPS) * gain_row


def _attn_kernel(q_ref, k_ref, v_ref, db_ref, qn_ref, kn_ref, lam_ref, subln_ref, o_ref,
                 k_s, vt_s, q_s, m_s, acc_s, *, lambda_init):
    tq = q_ref.shape[1]
    tk = db_ref.shape[2]
    T = k_ref.shape[1]
    dv = v_ref.shape[2]
    qi = pl.program_id(2)

    @pl.when(qi == 0)
    def _():
        ones_rows = jnp.where(lax.broadcasted_iota(jnp.int32, (vt_s.shape[1] - dv, tk), 0) == 0, 1.0, 0.0)

        def prep(i, carry):
            tiles = [i * ATTN_PREP_TILES + u for u in range(ATTN_PREP_TILES)]
            rows = [pl.ds(pl.multiple_of(j * tk, tk), tk) for j in tiles]
            kn = [_half_rmsnorm(k_ref[0, r, :], kn_ref[...]) for r in rows]
            vt = [v_ref[0, r, :].T for r in rows]
            for j, k, v in zip(tiles, kn, vt):
                k_s[j] = k.astype(bf16)
                vt_s[j] = jnp.concatenate([v, ones_rows], axis=0).astype(bf16)
            return carry
        lax.fori_loop(0, T // (tk * ATTN_PREP_TILES), prep, 0)

    qt = (_half_rmsnorm(q_ref[0], qn_ref[...]) * (DIFF_DH ** -0.5 * LOG2E)).T
    dim = lax.broadcasted_iota(jnp.int32, qt.shape, 0)
    q_s[0] = jnp.where(dim < DIFF_DH, qt, 0.0).astype(bf16)
    q_s[1] = jnp.where(dim < DIFF_DH, 0.0, qt).astype(bf16)

    r = tq // tk
    n_near = db_ref.shape[0]
    near_tiles = [r * (qi + 1) - 1 - t for t in range(n_near)]
    near_bias = [db_ref[t, 0] for t in range(n_near)]
    n_far = jnp.maximum(r * qi - 1, 0)
    n_grouped = (n_far // ATTN_GROUP) * ATTN_GROUP

    def unshifted(js, biases, acc):
        units = [(t, c) for t in range(len(js)) for c in range(2)]
        scores = {}
        for i in range(len(units) + ATTN_LOOKAHEAD):
            if i < len(units):
                t, c = units[i]
                s = jnp.dot(k_s[js[t]], q_s[c], preferred_element_type=f32)
                scores[i] = s if biases[t] is None else s + biases[t][:, c * tq:(c + 1) * tq]
            if i >= ATTN_LOOKAHEAD:
                t, c = units[i - ATTN_LOOKAHEAD]
                pv = jnp.dot(vt_s[js[t]], jnp.exp2(scores.pop(i - ATTN_LOOKAHEAD)).astype(bf16),
                             preferred_element_type=f32)
                acc[c] = pv if acc[c] is None else acc[c] + pv
        return acc

    @pl.when(qi == 0)
    def _():
        acc = unshifted(near_tiles[:r], near_bias[:r], [None, None])
        for c in range(2):
            acc_s[c] = acc[c]

    for rem in range(ATTN_GROUP):
        @pl.when(jnp.logical_and(qi >= 1, n_far - n_grouped == rem))
        def _():
            acc = unshifted(near_tiles + [n_grouped + t for t in range(rem)],
                            near_bias + [None] * rem, [None, None])
            for c in range(2):
                acc_s[c] = acc[c]

    def far_group(i, carry):
        acc = unshifted([ATTN_GROUP * i + t for t in range(ATTN_GROUP)], [None] * ATTN_GROUP,
                        [acc_s[0], acc_s[1]])
        for c in range(2):
            acc_s[c] = acc[c]
        return carry
    lax.fori_loop(0, n_far // ATTN_GROUP, far_group, 0)

    denom = jnp.concatenate([acc_s[0, dv:dv + 1, :], acc_s[1, dv:dv + 1, :]], axis=1)
    trusted = jnp.logical_and(jnp.min(denom) >= SOFTMAX_DENOM_MIN, jnp.max(denom) <= SOFTMAX_DENOM_MAX)

    @pl.when(jnp.logical_not(trusted))
    def _():
        m_s[...] = jnp.full_like(m_s, NEG_BIG)
        acc_s[...] = jnp.zeros_like(acc_s)

        def online(js, biases):
            for c in range(2):
                ss = []
                for j, bias in zip(js, biases):
                    s = jnp.dot(k_s[j], q_s[c], preferred_element_type=f32)
                    ss.append(s if bias is None else s + bias[:, c * tq:(c + 1) * tq])
                m_old = m_s[c]
                m_new = m_old
                for s in ss:
                    m_new = jnp.maximum(m_new, jnp.max(s, axis=0, keepdims=True))
                acc = jnp.exp2(m_old - m_new) * acc_s[c]
                for j, s in zip(js, ss):
                    acc = acc + jnp.dot(vt_s[j], jnp.exp2(s - m_new).astype(bf16), preferred_element_type=f32)
                acc_s[c] = acc
                m_s[c] = m_new

        for t in range(r):
            online([near_tiles[t]], [near_bias[t]])

        @pl.when(qi >= 1)
        def _():
            online([near_tiles[r]], [near_bias[r]])

        def far_tile(j, carry):
            online([j], [None])
            return carry
        lax.fori_loop(0, n_far, far_tile, 0)

    lam_rows = lam_ref[...]
    lam = (jnp.exp(jnp.sum(lam_rows[0:1] * lam_rows[1:2], axis=-1, keepdims=True))
           - jnp.exp(jnp.sum(lam_rows[2:3] * lam_rows[3:4], axis=-1, keepdims=True)) + lambda_init)
    acc1 = acc_s[0]
    acc2 = acc_s[1]
    ot = acc1[:dv] * (1.0 / acc1[dv:dv + 1]) - lam * (acc2[:dv] * (1.0 / acc2[dv:dv + 1]))
    ont = ot * lax.rsqrt(jnp.mean(ot * ot, axis=0, keepdims=True) + EPS)
    o_ref[0] = (ont.T * (subln_ref[...] * (1.0 - lambda_init))).astype(o_ref.dtype)


def _diff_attention(proj, db, q_norm, k_norm, lq1, lk1, lq2, lk2, subln, lambda_init):
    B, T, _ = proj.shape
    tq, tk = ATTN_TQ, ATTN_TK
    two = lambda v: jnp.concatenate([v, v]).reshape(1, HEAD_W).astype(f32)
    lam_rows = jnp.zeros((8, DIFF_DH), f32).at[:4].set(jnp.stack([lq1, lk1, lq2, lk2]).astype(f32))
    qb, kb, vb = COL_DQ // HEAD_W, COL_DK // HEAD_W, COL_DV // HEAD_W
    return pl.pallas_call(
        functools.partial(_attn_kernel, lambda_init=lambda_init),
        out_shape=jax.ShapeDtypeStruct((B, T, GROUP_W), bf16),
        grid=(B, DIFF_HEADS, T // tq),
        in_specs=[
            pl.BlockSpec((1, tq, HEAD_W), lambda b, h, i: (b, i, qb + h)),
            pl.BlockSpec((1, T, HEAD_W), lambda b, h, i: (b, 0, kb + h)),
            pl.BlockSpec((1, T, HEAD_W), lambda b, h, i: (b, 0, vb + h)),
            pl.BlockSpec((tq // tk + 1, 1, tk, 2 * tq), lambda b, h, i: (0, h, 0, 0)),
            _const_spec((1, HEAD_W)),
            _const_spec((1, HEAD_W)),
            _const_spec((8, DIFF_DH)),
            _const_spec((1, HEAD_W)),
        ],
        out_specs=pl.BlockSpec((1, tq, HEAD_W), lambda b, h, i: (b, i, h)),
        scratch_shapes=[
            pltpu.VMEM((T // tk, tk, HEAD_W), bf16),
            pltpu.VMEM((T // tk, HEAD_W + BF16_ROWS, tk), bf16),
            pltpu.VMEM((2, HEAD_W, tq), bf16),
            pltpu.VMEM((2, 1, tq), f32),
            pltpu.VMEM((2, HEAD_W + BF16_ROWS, tq), f32),
        ],
        compiler_params=_params(("arbitrary", "arbitrary", "arbitrary")),
        name="diff_attention",
    )(proj, proj, proj, db, two(q_norm), two(k_norm), lam_rows, subln.reshape(1, HEAD_W).astype(f32))


def _outffn_kernel(x_ref, yg_ref, yd_ref, mod_ref, g_ref, wout_ref, wup_ref, cw_ref, wdown_ref, o_ref,
                   h_s, acc_s, carry_s):
    nchunks = wup_ref.shape[0]
    fc = wdown_ref.shape[1]

    @pl.when(pl.program_id(1) == 0)
    def _():
        carry_s[...] = jnp.zeros_like(carry_s)

    half = yg_ref.shape[2]
    y = (jnp.dot(yg_ref[0], wout_ref[:half, :], preferred_element_type=f32)
         + jnp.dot(yd_ref[0], wout_ref[half:, :], preferred_element_type=f32))
    x1 = x_ref[0] + mod_ref[0, 2:3, :] * y
    o_ref[0] = x1
    h_s[...] = _modulated_norm(x1, g_ref[...], mod_ref, 3, 4).astype(bf16)
    acc_s[...] = jnp.zeros_like(acc_s)
    tm = h_s.shape[0]

    tr = tm // FFN_ROW_SPLIT

    def up(j):
        return [jnp.dot(h_s[r0:r0 + tr, :], wup_ref[j], preferred_element_type=f32)
                for r0 in range(0, tm, tr)]

    us = up(0)
    held = [[] for _ in range(FFN_ROW_SPLIT)]
    for j in range(nchunks):
        us_next = up(j + 1) if j + 1 < nchunks else None
        cw = cw_ref.at[j]
        prev8 = carry_s[j]
        flush = (j + 1) % FFN_DOWN_GROUP == 0 or j + 1 == nchunks
        for i, u in enumerate(us):
            yc = _causal_conv(u, prev8, cw, FFN_CONV) + cw[FFN_CONV:FFN_CONV + 1, :]
            prev8 = u[tr - 8:, :]
            held[i].append((_silu(yc[:, fc:]) * yc[:, :fc]).astype(bf16))
            if flush:
                j0 = j + 1 - len(held[i])
                w_down = jnp.concatenate([wdown_ref[jj] for jj in range(j0, j + 1)], axis=0)
                acc_s[i * tr:(i + 1) * tr, :] += jnp.dot(jnp.concatenate(held[i], axis=1), w_down,
                                                         preferred_element_type=f32)
                held[i] = []
        carry_s[j] = prev8
        us = us_next
    o_ref[0] = o_ref[0] + mod_ref[0, 5:6, :] * acc_s[...]


def _out_ffn(x, y_gdn, y_diff, mod_l, gain, w_out, w_up_c, conv_c, w_down_c):
    B, T, D = x.shape
    tm = TOKEN_TILE
    nchunks, _, fc2 = w_up_c.shape
    fc = fc2 // 2
    return pl.pallas_call(
        _outffn_kernel,
        out_shape=jax.ShapeDtypeStruct((B, T, D), f32),
        grid=(B, T // tm),
        in_specs=[
            pl.BlockSpec((1, tm, D), lambda b, t: (b, t, 0)),
            pl.BlockSpec((1, tm, GROUP_W), lambda b, t: (b, t, 0)),
            pl.BlockSpec((1, tm, GROUP_W), lambda b, t: (b, t, 0)),
            pl.BlockSpec((1, 6, D), lambda b, t: (b, 0, 0)),
            _const_spec((1, D)),
            _const_spec((2 * GROUP_W, D)),
            _const_spec((nchunks, D, fc2)),
            _const_spec((nchunks, 8, fc2)),
            _const_spec((nchunks, fc, D)),
        ],
        out_specs=pl.BlockSpec((1, tm, D), lambda b, t: (b, t, 0)),
        scratch_shapes=[
            pltpu.VMEM((tm, D), bf16),
            pltpu.VMEM((tm, D), f32),
            pltpu.VMEM((nchunks, 8, fc2), f32),
        ],
        compiler_params=_params(("arbitrary", "arbitrary")),
        name="out_ffn",
    )(x, y_gdn, y_diff, mod_l, gain.reshape(1, D), w_out, w_up_c, conv_c, w_down_c)


def _regroup_w_in(w_in_l):
    D = w_in_l.shape[0]
    o_b = 4 * GROUP_W
    o_d = o_b + 2 * GDN_HEADS
    small = jnp.zeros((D, HEAD_W), w_in_l.dtype).at[:, :2 * GDN_HEADS].set(w_in_l[:, o_b:o_d])
    return jnp.concatenate([w_in_l[:, :o_b], w_in_l[:, o_d:], small], axis=1).astype(bf16)


def _chunk_ffn(ffn_up_l, conv_w_l, conv_b_l, ffn_down_l, fc):
    D, two_f = ffn_up_l.shape
    F = two_f // 2
    n = F // fc
    pair = lambda m: jnp.concatenate([m[..., :F].reshape(m.shape[:-1] + (n, fc)),
                                      m[..., F:].reshape(m.shape[:-1] + (n, fc))], axis=-1)
    w_up_c = jnp.moveaxis(pair(ffn_up_l), -2, 0).astype(bf16)
    conv_rows = jnp.concatenate([conv_w_l.astype(f32), conv_b_l.astype(f32)[None]], axis=0)
    conv_c = jnp.moveaxis(pair(conv_rows), -2, 0)
    conv_c = jnp.zeros((n, 8, 2 * fc), f32).at[:, :FFN_CONV + 1].set(conv_c)
    w_down_c = ffn_down_l.reshape(n, fc, ffn_down_l.shape[1]).astype(bf16)
    return w_up_c, conv_c, w_down_c


def kernel(x, c, w_ada, b_ada, norm_mix, norm_ffn, w_in, gdn_conv_w, gdn_a_log, gdn_dt_bias, gdn_out_norm,
           diff_q_norm, diff_k_norm, diff_lambda_q1, diff_lambda_k1, diff_lambda_q2, diff_lambda_k2, diff_subln,
           rel_bias, w_out, ffn_up, ffn_conv_w, ffn_conv_b, ffn_down):
    B, T, D = x.shape
    depth = w_in.shape[0]
    mod = _ada_modulation(c, w_ada, b_ada).reshape(depth, B, 6, D)
    db = _bias_tiles(rel_bias, ATTN_TQ, ATTN_TK)
    for l in range(depth):
        proj = _in_projection(x, mod[l], norm_mix[l], _regroup_w_in(w_in[l]))
        y_gdn = _gdn_mixer(proj, gdn_conv_w[l], gdn_a_log[l], gdn_dt_bias[l], gdn_out_norm[l])
        lambda_init = 0.8 - 0.6 * math.exp(-0.3 * l)
        y_diff = _diff_attention(proj, db, diff_q_norm[l], diff_k_norm[l], diff_lambda_q1[l], diff_lambda_k1[l],
                                 diff_lambda_q2[l], diff_lambda_k2[l], diff_subln[l], lambda_init)
        w_up_c, conv_c, w_down_c = _chunk_ffn(ffn_up[l], ffn_conv_w[l], ffn_conv_b[l], ffn_down[l], FFN_CHUNK)
        x = _out_ffn(x, y_gdn, y_diff, mod[l], norm_ffn[l], w_out[l].astype(bf16), w_up_c, conv_c, w_down_c)
    return x
```

```python
import functools
import math

import numpy as np
import jax
import jax.numpy as jnp
from jax import lax
from jax.experimental import pallas as pl
from jax.experimental.pallas import tpu as pltpu

f32 = jnp.float32
bf16 = jnp.bfloat16

EPS = 1e-6
GDN_HEADS = 4
GDN_DK = 128
GDN_CONV = 4
GDN_CHUNK = 64
GDN_LOCKSTEP = 4
DIFF_HEADS = 4
DIFF_DH = 64
REL_BUCKETS = 32
REL_MAX_DIST = 128
FFN_CONV = 3
HEAD_W = 128
GROUP_W = GDN_HEADS * HEAD_W
NEG_BIG = -1e30
LOG2E = math.log2(math.e)
BF16_ROWS = 16

VMEM_LIMIT_BYTES = 56 * 1024 * 1024

COL_GQ, COL_GK, COL_GV, COL_GATE, COL_DQ, COL_DK, COL_DV, COL_BA = (
    0, GROUP_W, 2 * GROUP_W, 3 * GROUP_W, 4 * GROUP_W, 5 * GROUP_W, 6 * GROUP_W, 7 * GROUP_W)
PROJ_W = 7 * GROUP_W + HEAD_W

TOKEN_TILE = 512
GDN_TOKEN_TILE = 1024
ATTN_TQ = 512
ATTN_TK = 256
ATTN_PREP_TILES = 4
ATTN_GROUP = 8
ATTN_LOOKAHEAD = 4
SOFTMAX_DENOM_MIN = 2.0 ** -64
SOFTMAX_DENOM_MAX = 2.0 ** 80
FFN_CHUNK = 256
FFN_DOWN_GROUP = 11
FFN_ROW_SPLIT = 2


def _bdot(a, b):
    return jnp.dot(a.astype(bf16), b.astype(bf16), preferred_element_type=f32)


def _bdot_nt(a, b):
    return lax.dot_general(a.astype(bf16), b.astype(bf16), (((1,), (1,)), ((), ())),
                           preferred_element_type=f32)


def _silu(x):
    return x * (1.0 / (1.0 + jnp.exp(-x)))


def _params(semantics):
    return pltpu.CompilerParams(dimension_semantics=semantics, vmem_limit_bytes=VMEM_LIMIT_BYTES)


def _const_spec(shape):
    nd = len(shape)
    return pl.BlockSpec(shape, lambda *_: (0,) * nd, pipeline_mode=pl.Buffered(1))


def _ada_kernel(c_ref, w_ref, b_ref, o_ref):
    ca = _silu(c_ref[...])
    o_ref[0] = _bdot(ca, w_ref[0]) + b_ref[0]


def _ada_modulation(c, w_ada, b_ada):
    L, D, N = w_ada.shape
    B = c.shape[0]
    tn = 1536
    return pl.pallas_call(
        _ada_kernel,
        out_shape=jax.ShapeDtypeStruct((L, B, N), f32),
        grid=(L, N // tn),
        in_specs=[
            pl.BlockSpec((B, D), lambda l, n: (0, 0)),
            pl.BlockSpec((1, D, tn), lambda l, n: (l, 0, n)),
            pl.BlockSpec((1, 1, tn), lambda l, n: (l, 0, n)),
        ],
        out_specs=pl.BlockSpec((1, B, tn), lambda l, n: (l, 0, n)),
        compiler_params=_params(("arbitrary", "arbitrary")),
        name="ada_modulation",
    )(c, w_ada, b_ada.reshape(L, 1, N))


def _bucket_of_distance(n):
    max_exact = REL_BUCKETS // 2
    nf = np.maximum(n, 1).astype(np.float32)
    large = max_exact + (np.log(nf / np.float32(max_exact)) / np.float32(math.log(REL_MAX_DIST / max_exact))
                         * np.float32(REL_BUCKETS - max_exact)).astype(np.int32)
    large = np.minimum(large, REL_BUCKETS - 1)
    return np.where(n < max_exact, n, large).astype(np.int32)


def _bias_bucket_tiles(tq, tk):
    assert tq % tk == 0
    r = tq // tk
    i = np.arange(tq)[None, :]
    j = np.arange(tk)[:, None]
    tiles = []
    for t in range(r + 1):
        dist = i - j + (t - (r - 1)) * tk
        tiles.append(np.where(dist >= 0, _bucket_of_distance(np.maximum(dist, 0)), -1))
    assert _bucket_of_distance(np.array([tk + 1]))[0] == REL_BUCKETS - 1
    near = np.stack(tiles).astype(np.int32)
    return np.concatenate([near, near], axis=2)


def _bias_tile_kernel(rel_ref, bk_ref, o_ref):
    h = pl.program_id(1)
    bk = bk_ref[0]
    last = rel_ref[REL_BUCKETS - 1, h]
    acc = jnp.where(bk < 0, NEG_BIG, 0.0).astype(f32)
    for b in range(REL_BUCKETS - 1):
        acc = jnp.where(bk == b, (rel_ref[b, h] - last) * LOG2E, acc)
    o_ref[0, 0] = acc


def _bias_tiles(rel_bias, tq, tk):
    buckets = jnp.asarray(_bias_bucket_tiles(tq, tk))
    return pl.pallas_call(
        _bias_tile_kernel,
        out_shape=jax.ShapeDtypeStruct((buckets.shape[0], DIFF_HEADS, tk, 2 * tq), f32),
        grid=(buckets.shape[0], DIFF_HEADS),
        in_specs=[
            pl.BlockSpec(memory_space=pltpu.SMEM),
            pl.BlockSpec((1, tk, 2 * tq), lambda k, h: (k, 0, 0)),
        ],
        out_specs=pl.BlockSpec((1, 1, tk, 2 * tq), lambda k, h: (k, h, 0, 0)),
        compiler_params=_params(("arbitrary", "arbitrary")),
        name="rel_bias_tiles",
    )(rel_bias.astype(f32), buckets)


def _modulated_norm(x, gain_row, mod_ref, shift_idx, scale_idx):
    ms = jnp.mean(x * x, axis=-1, keepdims=True)
    y = x * lax.rsqrt(ms + EPS)
    return y * (gain_row * (1.0 + mod_ref[0, scale_idx:scale_idx + 1, :])) + mod_ref[0, shift_idx:shift_idx + 1, :]


def _inproj_kernel(x_ref, mod_ref, g_ref, w_ref, o_ref, h_scr):
    h_scr[...] = _modulated_norm(x_ref[0], g_ref[...], mod_ref, 0, 1).astype(bf16)
    n = o_ref.shape[2]
    for c0 in range(0, n, GROUP_W):
        c1 = min(c0 + GROUP_W, n)
        o_ref[0, :, c0:c1] = jnp.dot(h_scr[...], w_ref[:, c0:c1], preferred_element_type=f32)


def _in_projection(x, mod_l, gain, w_perm):
    B, T, D = x.shape
    tm = TOKEN_TILE
    return pl.pallas_call(
        _inproj_kernel,
        out_shape=jax.ShapeDtypeStruct((B, T, PROJ_W), f32),
        grid=(B, T // tm),
        in_specs=[
            pl.BlockSpec((1, tm, D), lambda b, t: (b, t, 0)),
            pl.BlockSpec((1, 6, D), lambda b, t: (b, 0, 0)),
            _const_spec((1, D)),
            _const_spec((D, PROJ_W)),
        ],
        out_specs=pl.BlockSpec((1, tm, PROJ_W), lambda b, t: (b, t, 0)),
        scratch_shapes=[pltpu.VMEM((tm, D), bf16)],
        compiler_params=_params(("arbitrary", "arbitrary")),
        name="in_projection",
    )(x, mod_l, gain.reshape(1, D), w_perm)


def _shift_rows(x, prev8, s):
    xs = pltpu.roll(x, s, axis=0)
    fix = pltpu.roll(prev8, s, axis=0)
    row = lax.broadcasted_iota(jnp.int32, fix.shape, 0)
    head = jnp.where(row < s, fix, xs[:8])
    return jnp.concatenate([head, xs[8:]], axis=0)


def _causal_conv(x, prev8, w_ref, taps):
    acc = x * w_ref[taps - 1:taps, :]
    for s in range(1, taps):
        acc = acc + _shift_rows(x, prev8, s) * w_ref[taps - 1 - s:taps - s, :]
    return acc


def _expand_block_diagonal(packed):
    c, n = packed.shape
    row = lax.broadcasted_iota(jnp.int32, (n, n), 0)
    col = lax.broadcasted_iota(jnp.int32, (n, n), 1)
    tiled = jnp.concatenate([packed] * (n // c), axis=0)
    return jnp.where((row // c) == (col // c), tiled, 0.0).astype(bf16)


def _packed_lower_inverses(lows, base, out):
    c, n = lows[0].shape
    row = lax.broadcasted_iota(jnp.int32, (c, n), 0)
    col = lax.broadcasted_iota(jnp.int32, (c, n), 1) % c
    same_block = lambda size: (row // size) == (col // size)
    eye = jnp.where(row == col, 1.0, 0.0).astype(f32)
    base_mask = same_block(base)
    lbs = [jnp.where(base_mask, low, 0.0) for low in lows]
    xs = [eye - lb for lb in lbs]
    ps = [jnp.dot(lb.astype(bf16), _expand_block_diagonal(lb), preferred_element_type=f32) for lb in lbs]
    yield
    for _ in range(int(math.log2(base)) - 2):
        xps = [jnp.dot(jnp.concatenate([x, p], axis=0).astype(bf16), _expand_block_diagonal(p),
                       preferred_element_type=f32) for x, p in zip(xs, ps)]
        yield
        xs = [x + xp[:c] for x, xp in zip(xs, xps)]
        ps = [xp[c:] for xp in xps]
    xs = [x + jnp.dot(x.astype(bf16), _expand_block_diagonal(p), preferred_element_type=f32)
          for x, p in zip(xs, ps)]
    yield
    size = base
    while size < c:
        off_mask = jnp.logical_and(same_block(2 * size), jnp.logical_not(same_block(size)))
        ts = [jnp.dot(x.astype(bf16), _expand_block_diagonal(jnp.where(off_mask, low, 0.0)),
                      preferred_element_type=f32) for x, low in zip(xs, lows)]
        yield
        ts = [jnp.dot(t.astype(bf16), _expand_block_diagonal(x), preferred_element_type=f32)
              for t, x in zip(ts, xs)]
        yield
        xs = [x - t for x, t in zip(xs, ts)]
        size *= 2
    out.extend(xs)


def _gdn_kernel(qkv_ref, gate_ref, ba_ref, convw_ref, alog_ref, dt_ref, onorm_ref, o_ref,
                q_s, k_s, v_s, gc_s, beta_s, state_s, carry_s):
    tg = qkv_ref.shape[1]
    C = GDN_CHUNK

    @pl.when(pl.program_id(1) == 0)
    def _():
        state_s[...] = jnp.zeros_like(state_s)
        carry_s[...] = jnp.zeros_like(carry_s)

    def token_prep(r0, nr):
        rows = slice(r0, r0 + nr)
        for grp, dst in enumerate((q_s, k_s, v_s)):
            for h in range(GDN_HEADS):
                c0 = grp * GROUP_W + h * HEAD_W
                cols = slice(c0, c0 + HEAD_W)
                prev8 = carry_s[:, cols] if r0 == 0 else qkv_ref[0, r0 - 8:r0, cols]
                y = _silu(_causal_conv(qkv_ref[0, rows, cols], prev8, convw_ref.at[:, cols], GDN_CONV))
                if grp < 2:
                    y = y * lax.rsqrt(jnp.sum(y * y, axis=-1, keepdims=True) + EPS)
                if grp == 0:
                    y = y * (GDN_DK ** -0.5)
                dst[rows, h * HEAD_W:(h + 1) * HEAD_W] = y
        ba = ba_ref[0, rows, :]
        beta_s[rows, :] = 1.0 / (1.0 + jnp.exp(-ba))
        z = ba + dt_ref[...]
        g = -jnp.exp(alog_ref[...]) * (jnp.maximum(z, 0.0) + jnp.log1p(jnp.exp(-jnp.abs(z))))
        rowc = lax.broadcasted_iota(jnp.int32, g.shape, 0) % C
        s = 1
        while s < C:
            g = g + jnp.where(rowc >= s, pltpu.roll(g, s, axis=0), 0.0)
            s *= 2
        gc_s[rows, :] = g

    H = GDN_HEADS
    n = H * C
    ri = lax.broadcasted_iota(jnp.int32, (C, n), 0)
    ci = lax.broadcasted_iota(jnp.int32, (C, n), 1)
    lane_head = ci // C
    incl = ri >= ci % C
    strict = ri > ci % C
    onorm = onorm_ref[...]

    def pack_diagonal_blocks(big):
        out = big[0:C]
        for h in range(1, H):
            out = jnp.where(lane_head == h, big[h * C:(h + 1) * C], out)
        return out

    def chunk_operands(c):
        rows = slice(c * C, (c + 1) * C)
        gcc = gc_s[rows, :]
        gct = gcc.T
        betac = beta_s[rows, :]
        glast = gc_s[(c + 1) * C - 1:(c + 1) * C, :]
        ks, kbs, qs, rhs, qgs, kdts, gcols, grows, egls = [], [], [], [], [], [], [], [], []
        for h in range(H):
            cols = slice(h * HEAD_W, (h + 1) * HEAD_W)
            q = q_s[rows, cols]
            k = k_s[rows, cols]
            v = v_s[rows, cols]
            beta = betac[:, h:h + 1]
            gcol = gcc[:, H + h:H + h + 1]
            gl = glast[:, H + h:H + h + 1]
            eg = jnp.exp(gcol)
            kb = k * beta
            ks.append(k)
            kbs.append(kb)
            qs.append(q)
            rhs.append(jnp.concatenate([v * beta, kb * eg], axis=1).astype(bf16))
            qgs.append((q * eg).astype(bf16))
            kdts.append((k * jnp.exp(gl - gcol)).T.astype(bf16))
            gcols.append(jnp.broadcast_to(gcol, (C, C)))
            grows.append(gct[H + h:H + h + 1, :])
            egls.append(jnp.exp(gl))
        gcol_all = jnp.concatenate(gcols, axis=1)
        grow_all = jnp.concatenate(grows, axis=1)
        decay = jnp.exp(jnp.where(incl, gcol_all - grow_all, NEG_BIG))
        return dict(kbq=jnp.concatenate(kbs + qs, axis=0).astype(bf16), k_all=jnp.concatenate(ks, axis=0).astype(bf16),
                    rhs=rhs, decay=decay, qg=qgs, kdt=kdts, egl=egls)

    def state_free_part(c0):
        ops = [chunk_operands(c) for c in range(c0, c0 + GDN_LOCKSTEP)]
        if c0 + GDN_LOCKSTEP < tg // C:
            token_prep((c0 + GDN_LOCKSTEP) * C, GDN_LOCKSTEP * C)
        kqs = [lax.dot_general(o["kbq"], o["k_all"], (((1,), (1,)), ((), ())), preferred_element_type=f32)
               for o in ops]
        yield
        lows = [jnp.where(strict, pack_diagonal_blocks(kq[:n]) * o["decay"], 0.0) for kq, o in zip(kqs, ops)]
        for kq, o in zip(kqs, ops):
            o["a"] = (pack_diagonal_blocks(kq[n:]) * o["decay"]).astype(bf16)
        t_invs = []
        yield from _packed_lower_inverses(lows, 16, t_invs)
        for t_inv, o in zip(t_invs, ops):
            t_inv = t_inv.astype(bf16)
            o["uw"] = [jnp.dot(t_inv[:, h * C:(h + 1) * C], o["rhs"][h], preferred_element_type=f32)
                       for h in range(H)]
        yield
        chunks.extend(ops)

    def recurrence(c_first, c_last):
        for c in range(c_first, c_last):
            o = chunks[c]
            rows = slice(c * C, (c + 1) * C)
            uw = o["uw"]
            wss = [jnp.dot(jnp.concatenate([uw[h][:, HEAD_W:].astype(bf16), o["qg"][h]], axis=0),
                           states[h].astype(bf16), preferred_element_type=f32) for h in range(H)]
            yield
            v_news = [uw[h][:, :HEAD_W] - wss[h][:C] for h in range(H)]
            for h in range(H):
                states[h] = (states[h] * o["egl"][h]
                             + jnp.dot(o["kdt"][h], v_news[h].astype(bf16), preferred_element_type=f32))
            yield
            outs = [wss[h][C:] + jnp.dot(o["a"][:, h * C:(h + 1) * C], v_news[h].astype(bf16),
                                         preferred_element_type=f32) for h in range(H)]
            yield
            for h in range(H):
                cols = slice(h * HEAD_W, (h + 1) * HEAD_W)
                oh = outs[h]
                on = oh * lax.rsqrt(jnp.mean(oh * oh, axis=-1, keepdims=True) + EPS) * onorm
                o_ref[0, rows, cols] = (on * _silu(gate_ref[0, rows, cols])).astype(o_ref.dtype)

    chunks = []
    states = [state_s[h] for h in range(H)]
    token_prep(0, GDN_LOCKSTEP * C)
    pending = iter(())
    for c0 in range(0, tg // C, GDN_LOCKSTEP):
        for _ in state_free_part(c0):
            next(pending, None)
        for _ in pending:
            pass
        pending = recurrence(c0, c0 + GDN_LOCKSTEP)
    for _ in pending:
        pass
    for h in range(H):
        state_s[h] = states[h]
    carry_s[...] = qkv_ref[0, tg - 8:tg, :]


def _gdn_mixer(proj, conv_w, a_log, dt_bias, out_norm):
    B, T, _ = proj.shape
    tg = GDN_TOKEN_TILE
    convw =jnp.zeros((8, 3 * GROUP_W), f32).at[:GDN_CONV].set(conv_w.astype(f32))
    lane_pad = lambda v: jnp.zeros((1, HEAD_W), f32).at[0, GDN_HEADS:2 * GDN_HEADS].set(v.astype(f32))
    return pl.pallas_call(
        _gdn_kernel,
        out_shape=jax.ShapeDtypeStruct((B, T, GROUP_W), bf16),
        grid=(B, T // tg),
        in_specs=[
            pl.BlockSpec((1, tg, 3 * GROUP_W), lambda b, t: (b, t, 0)),
            pl.BlockSpec((1, tg, GROUP_W), lambda b, t: (b, t, COL_GATE // GROUP_W)),
            pl.BlockSpec((1, tg, HEAD_W), lambda b, t: (b, t, COL_BA // HEAD_W)),
            _const_spec((8, 3 * GROUP_W)),
            _const_spec((1, HEAD_W)),
            _const_spec((1, HEAD_W)),
            _const_spec((1, HEAD_W)),
        ],
        out_specs=pl.BlockSpec((1, tg, GROUP_W), lambda b, t: (b, t, 0)),
        scratch_shapes=[
            pltpu.VMEM((tg, GROUP_W), f32),
            pltpu.VMEM((tg, GROUP_W), f32),
            pltpu.VMEM((tg, GROUP_W), f32),
            pltpu.VMEM((tg, HEAD_W), f32),
            pltpu.VMEM((tg, HEAD_W), f32),
            pltpu.VMEM((GDN_HEADS, GDN_DK, HEAD_W), f32),
            pltpu.VMEM((8, 3 * GROUP_W), f32),
        ],
        compiler_params=_params(("arbitrary", "arbitrary")),
        name="gdn_mixer",
    )(proj, proj, proj, convw, lane_pad(a_log), lane_pad(dt_bias), out_norm.reshape(1, HEAD_W).astype(f32))


def _half_rmsnorm(x, gain_row):
    w = x.shape[1]
    ri = lax.broadcasted_iota(jnp.int32, (w, w), 0)
    ci = lax.broadcasted_iota(jnp.int32, (w, w), 1)
    same_half = jnp.where((ri // DIFF_DH) == (ci // DIFF_DH), 1.0, 0.0).astype(bf16)
    x2 = x * x
    hi = x2.astype(bf16)
    lo = (x2 - hi.astype(f32)).astype(bf16)
    ss = (jnp.dot(hi, same_half, preferred_element_type=f32)
          + jnp.dot(lo, same_half, preferred_element_type=f32))
    return x * lax.rsqrt(ss * (1.0 / DIFF_DH) + EPS) * gain_row


def _attn_kernel(q_ref, k_ref, v_ref, db_ref, qn_ref, kn_ref, lam_ref, subln_ref, o_ref,
                 k_s, vt_s, q_s, m_s, acc_s, *, lambda_init):
    tq = q_ref.shape[1]
    tk = db_ref.shape[2]
    T = k_ref.shape[1]
    dv = v_ref.shape[2]
    qi = pl.program_id(2)

    @pl.when(qi == 0)
    def _():
        ones_rows = jnp.where(lax.broadcasted_iota(jnp.int32, (vt_s.shape[1] - dv, tk), 0) == 0, 1.0, 0.0)

        def prep(i, carry):
            tiles = [i * ATTN_PREP_TILES + u for u in range(ATTN_PREP_TILES)]
            rows = [pl.ds(pl.multiple_of(j * tk, tk), tk) for j in tiles]
            kn = [_half_rmsnorm(k_ref[0, r, :], kn_ref[...]) for r in rows]
            vt = [v_ref[0, r, :].T for r in rows]
            for j, k, v in zip(tiles, kn, vt):
                k_s[j] = k.astype(bf16)
                vt_s[j] = jnp.concatenate([v, ones_rows], axis=0).astype(bf16)
            return carry
        lax.fori_loop(0, T // (tk * ATTN_PREP_TILES), prep, 0)

    qt = (_half_rmsnorm(q_ref[0], qn_ref[...]) * (DIFF_DH ** -0.5 * LOG2E)).T
    dim = lax.broadcasted_iota(jnp.int32, qt.shape, 0)
    q_s[0] = jnp.where(dim < DIFF_DH, qt, 0.0).astype(bf16)
    q_s[1] = jnp.where(dim < DIFF_DH, 0.0, qt).astype(bf16)

    r = tq // tk
    n_near = db_ref.shape[0]
    near_tiles = [r * (qi + 1) - 1 - t for t in range(n_near)]
    near_bias = [db_ref[t, 0] for t in range(n_near)]
    n_far = jnp.maximum(r * qi - 1, 0)
    n_grouped = (n_far // ATTN_GROUP) * ATTN_GROUP

    def unshifted(js, biases, acc):
        units = [(t, c) for t in range(len(js)) for c in range(2)]
        scores = {}
        for i in range(len(units) + ATTN_LOOKAHEAD):
            if i < len(units):
                t, c = units[i]
                s = jnp.dot(k_s[js[t]], q_s[c], preferred_element_type=f32)
                scores[i] = s if biases[t] is None else s + biases[t][:, c * tq:(c + 1) * tq]
            if i >= ATTN_LOOKAHEAD:
                t, c = units[i - ATTN_LOOKAHEAD]
                pv = jnp.dot(vt_s[js[t]], jnp.exp2(scores.pop(i - ATTN_LOOKAHEAD)).astype(bf16),
                             preferred_element_type=f32)
                acc[c] = pv if acc[c] is None else acc[c] + pv
        return acc

    @pl.when(qi == 0)
    def _():
        acc = unshifted(near_tiles[:r], near_bias[:r], [None, None])
        for c in range(2):
            acc_s[c] = acc[c]

    for rem in range(ATTN_GROUP):
        @pl.when(jnp.logical_and(qi >= 1, n_far - n_grouped == rem))
        def _():
            acc = unshifted(near_tiles + [n_grouped + t for t in range(rem)],
                            near_bias + [None] * rem, [None, None])
            for c in range(2):
                acc_s[c] = acc[c]

    def far_group(i, carry):
        acc = unshifted([ATTN_GROUP * i + t for t in range(ATTN_GROUP)], [None] * ATTN_GROUP,
                        [acc_s[0], acc_s[1]])
        for c in range(2):
            acc_s[c] = acc[c]
        return carry
    lax.fori_loop(0, n_far // ATTN_GROUP, far_group, 0)

    denom = jnp.concatenate([acc_s[0, dv:dv + 1, :], acc_s[1, dv:dv + 1, :]], axis=1)
    trusted = jnp.logical_and(jnp.min(denom) >= SOFTMAX_DENOM_MIN, jnp.max(denom) <= SOFTMAX_DENOM_MAX)

    @pl.when(jnp.logical_not(trusted))
    def _():
        m_s[...] = jnp.full_like(m_s, NEG_BIG)
        acc_s[...] = jnp.zeros_like(acc_s)

        def online(js, biases):
            for c in range(2):
                ss = []
                for j, bias in zip(js, biases):
                    s = jnp.dot(k_s[j], q_s[c], preferred_element_type=f32)
                    ss.append(s if bias is None else s + bias[:, c * tq:(c + 1) * tq])
                m_old = m_s[c]
                m_new = m_old
                for s in ss:
                    m_new = jnp.maximum(m_new, jnp.max(s, axis=0, keepdims=True))
                acc = jnp.exp2(m_old - m_new) * acc_s[c]
                for j, s in zip(js, ss):
                    acc = acc + jnp.dot(vt_s[j], jnp.exp2(s - m_new).astype(bf16), preferred_element_type=f32)
                acc_s[c] = acc
                m_s[c] = m_new

        for t in range(r):
            online([near_tiles[t]], [near_bias[t]])

        @pl.when(qi >= 1)
        def _():
            online([near_tiles[r]], [near_bias[r]])

        def far_tile(j, carry):
            online([j], [None])
            return carry
        lax.fori_loop(0, n_far, far_tile, 0)

    lam_rows = lam_ref[...]
    lam = (jnp.exp(jnp.sum(lam_rows[0:1] * lam_rows[1:2], axis=-1, keepdims=True))
           - jnp.exp(jnp.sum(lam_rows[2:3] * lam_rows[3:4], axis=-1, keepdims=True)) + lambda_init)
    acc1 = acc_s[0]
    acc2 = acc_s[1]
    ot = acc1[:dv] * (1.0 / acc1[dv:dv + 1]) - lam * (acc2[:dv] * (1.0 / acc2[dv:dv + 1]))
    ont = ot * lax.rsqrt(jnp.mean(ot * ot, axis=0, keepdims=True) + EPS)
    o_ref[0] = (ont.T * (subln_ref[...] * (1.0 - lambda_init))).astype(o_ref.dtype)


def _diff_attention(proj, db, q_norm, k_norm, lq1, lk1, lq2, lk2, subln, lambda_init):
    B, T, _ = proj.shape
    tq, tk = ATTN_TQ, ATTN_TK
    two = lambda v: jnp.concatenate([v, v]).reshape(1, HEAD_W).astype(f32)
    lam_rows = jnp.zeros((8, DIFF_DH), f32).at[:4].set(jnp.stack([lq1, lk1, lq2, lk2]).astype(f32))
    qb, kb, vb = COL_DQ // HEAD_W, COL_DK // HEAD_W, COL_DV // HEAD_W
    return pl.pallas_call(
        functools.partial(_attn_kernel, lambda_init=lambda_init),
        out_shape=jax.ShapeDtypeStruct((B, T, GROUP_W), bf16),
        grid=(B, DIFF_HEADS, T // tq),
        in_specs=[
            pl.BlockSpec((1, tq, HEAD_W), lambda b, h, i: (b, i, qb + h)),
            pl.BlockSpec((1, T, HEAD_W), lambda b, h, i: (b, 0, kb + h)),
            pl.BlockSpec((1, T, HEAD_W), lambda b, h, i: (b, 0, vb + h)),
            pl.BlockSpec((tq // tk + 1, 1, tk, 2 * tq), lambda b, h, i: (0, h, 0, 0)),
            _const_spec((1, HEAD_W)),
            _const_spec((1, HEAD_W)),
            _const_spec((8, DIFF_DH)),
            _const_spec((1, HEAD_W)),
        ],
        out_specs=pl.BlockSpec((1, tq, HEAD_W), lambda b, h, i: (b, i, h)),
        scratch_shapes=[
            pltpu.VMEM((T // tk, tk, HEAD_W), bf16),
            pltpu.VMEM((T // tk, HEAD_W + BF16_ROWS, tk), bf16),
            pltpu.VMEM((2, HEAD_W, tq), bf16),
            pltpu.VMEM((2, 1, tq), f32),
            pltpu.VMEM((2, HEAD_W + BF16_ROWS, tq), f32),
        ],
        compiler_params=_params(("arbitrary", "arbitrary", "arbitrary")),
        name="diff_attention",
    )(proj, proj, proj, db, two(q_norm), two(k_norm), lam_rows, subln.reshape(1, HEAD_W).astype(f32))


def _outffn_kernel(x_ref, yg_ref, yd_ref, mod_ref, g_ref, wout_ref, wup_ref, cw_ref, wdown_ref, o_ref,
                   h_s, acc_s, carry_s):
    nchunks = wup_ref.shape[0]
    fc = wdown_ref.shape[1]

    @pl.when(pl.program_id(1) == 0)
    def _():
        carry_s[...] = jnp.zeros_like(carry_s)

    half = yg_ref.shape[2]
    y = (jnp.dot(yg_ref[0], wout_ref[:half, :], preferred_element_type=f32)
         + jnp.dot(yd_ref[0], wout_ref[half:, :], preferred_element_type=f32))
    x1 = x_ref[0] + mod_ref[0, 2:3, :] * y
    o_ref[0] = x1
    h_s[...] = _modulated_norm(x1, g_ref[...], mod_ref, 3, 4).astype(bf16)
    acc_s[...] = jnp.zeros_like(acc_s)
    tm = h_s.shape[0]

    tr = tm // FFN_ROW_SPLIT

    def up(j):
        return [jnp.dot(h_s[r0:r0 + tr, :], wup_ref[j], preferred_element_type=f32)
                for r0 in range(0, tm, tr)]

    us = up(0)
    held = [[] for _ in range(FFN_ROW_SPLIT)]
    for j in range(nchunks):
        us_next = up(j + 1) if j + 1 < nchunks else None
        cw = cw_ref.at[j]
        prev8 = carry_s[j]
        flush = (j + 1) % FFN_DOWN_GROUP == 0 or j + 1 == nchunks
        for i, u in enumerate(us):
            yc = _causal_conv(u, prev8, cw, FFN_CONV) + cw[FFN_CONV:FFN_CONV + 1, :]
            prev8 = u[tr - 8:, :]
            held[i].append((_silu(yc[:, fc:]) * yc[:, :fc]).astype(bf16))
            if flush:
                j0 = j + 1 - len(held[i])
                w_down = jnp.concatenate([wdown_ref[jj] for jj in range(j0, j + 1)], axis=0)
                acc_s[i * tr:(i + 1) * tr, :] += jnp.dot(jnp.concatenate(held[i], axis=1), w_down,
                                                         preferred_element_type=f32)
                held[i] = []
        carry_s[j] = prev8
        us = us_next
    o_ref[0] = o_ref[0] + mod_ref[0, 5:6, :] * acc_s[...]


def _out_ffn(x, y_gdn, y_diff, mod_l, gain, w_out, w_up_c, conv_c, w_down_c):
    B, T, D = x.shape
    tm = TOKEN_TILE
    nchunks, _, fc2 = w_up_c.shape
    fc = fc2 // 2
    return pl.pallas_call(
        _outffn_kernel,
        out_shape=jax.ShapeDtypeStruct((B, T, D), f32),
        grid=(B, T // tm),
        in_specs=[
            pl.BlockSpec((1, tm, D), lambda b, t: (b, t, 0)),
            pl.BlockSpec((1, tm, GROUP_W), lambda b, t: (b, t, 0)),
            pl.BlockSpec((1, tm, GROUP_W), lambda b, t: (b, t, 0)),
            pl.BlockSpec((1, 6, D), lambda b, t: (b, 0, 0)),
            _const_spec((1, D)),
            _const_spec((2 * GROUP_W, D)),
            _const_spec((nchunks, D, fc2)),
            _const_spec((nchunks, 8, fc2)),
            _const_spec((nchunks, fc, D)),
        ],
        out_specs=pl.BlockSpec((1, tm, D), lambda b, t: (b, t, 0)),
        scratch_shapes=[
            pltpu.VMEM((tm, D), bf16),
            pltpu.VMEM((tm, D), f32),
            pltpu.VMEM((nchunks, 8, fc2), f32),
        ],
        compiler_params=_params(("arbitrary", "arbitrary")),
        name="out_ffn",
    )(x, y_gdn, y_diff, mod_l, gain.reshape(1, D), w_out, w_up_c, conv_c, w_down_c)


def _regroup_w_in(w_in_l):
    D = w_in_l.shape[0]
    o_b = 4 * GROUP_W
    o_d = o_b + 2 * GDN_HEADS
    small = jnp.zeros((D, HEAD_W), w_in_l.dtype).at[:, :2 * GDN_HEADS].set(w_in_l[:, o_b:o_d])
    return jnp.concatenate([w_in_l[:, :o_b], w_in_l[:, o_d:], small], axis=1).astype(bf16)


def _chunk_ffn(ffn_up_l, conv_w_l, conv_b_l, ffn_down_l, fc):
    D, two_f = ffn_up_l.shape
    F = two_f // 2
    n = F // fc
    pair = lambda m: jnp.concatenate([m[..., :F].reshape(m.shape[:-1] + (n, fc)),
                                      m[..., F:].reshape(m.shape[:-1] + (n, fc))], axis=-1)
    w_up_c = jnp.moveaxis(pair(ffn_up_l), -2, 0).astype(bf16)
    conv_rows = jnp.concatenate([conv_w_l.astype(f32), conv_b_l.astype(f32)[None]], axis=0)
    conv_c = jnp.moveaxis(pair(conv_rows), -2, 0)
    conv_c = jnp.zeros((n, 8, 2 * fc), f32).at[:, :FFN_CONV + 1].set(conv_c)
    w_down_c = ffn_down_l.reshape(n, fc, ffn_down_l.shape[1]).astype(bf16)
    return w_up_c, conv_c, w_down_c


def kernel(x, c, w_ada, b_ada, norm_mix, norm_ffn, w_in, gdn_conv_w, gdn_a_log, gdn_dt_bias, gdn_out_norm,
           diff_q_norm, diff_k_norm, diff_lambda_q1, diff_lambda_k1, diff_lambda_q2, diff_lambda_k2, diff_subln,
           rel_bias, w_out, ffn_up, ffn_conv_w, ffn_conv_b, ffn_down):
    B, T, D = x.shape
    depth = w_in.shape[0]
    mod = _ada_modulation(c, w_ada, b_ada).reshape(depth, B, 6, D)
    db = _bias_tiles(rel_bias, ATTN_TQ, ATTN_TK)
    for l in range(depth):
        proj = _in_projection(x, mod[l], norm_mix[l], _regroup_w_in(w_in[l]))
        y_gdn = _gdn_mixer(proj, gdn_conv_w[l], gdn_a_log[l], gdn_dt_bias[l], gdn_out_norm[l])
        lambda_init = 0.8 - 0.6 * math.exp(-0.3 * l)
        y_diff = _diff_attention(proj, db, diff_q_norm[l], diff_k_norm[l], diff_lambda_q1[l], diff_lambda_k1[l],
                                 diff_lambda_q2[l], diff_lambda_k2[l], diff_subln[l], lambda_init)
        w_up_c, conv_c, w_down_c = _chunk_ffn(ffn_up[l], ffn_conv_w[l], ffn_conv_b[l], ffn_down[l], FFN_CHUNK)
        x = _out_ffn(x, y_gdn, y_diff, mod[l], norm_ffn[l], w_out[l].astype(bf16), w_up_c, conv_c, w_down_c)
    return x
```

```python
import functools
import math

import numpy as np
import jax
import jax.numpy as jnp
from jax import lax
from jax.experimental import pallas as pl
from jax.experimental.pallas import tpu as pltpu

f32 = jnp.float32
bf16 = jnp.bfloat16

EPS = 1e-6
GDN_HEADS = 4
GDN_DK = 128
GDN_CONV = 4
GDN_CHUNK = 64
GDN_LOCKSTEP = 4
DIFF_HEADS = 4
DIFF_DH = 64
REL_BUCKETS = 32
REL_MAX_DIST = 128
FFN_CONV = 3
HEAD_W = 128
GROUP_W = GDN_HEADS * HEAD_W
NEG_BIG = -1e30
LOG2E = math.log2(math.e)
BF16_ROWS = 16

VMEM_LIMIT_BYTES = 56 * 1024 * 1024

COL_GQ, COL_GK, COL_GV, COL_GATE, COL_DQ, COL_DK, COL_DV, COL_BA = (
    0, GROUP_W, 2 * GROUP_W, 3 * GROUP_W, 4 * GROUP_W, 5 * GROUP_W, 6 * GROUP_W, 7 * GROUP_W)
PROJ_W = 7 * GROUP_W + HEAD_W

TOKEN_TILE = 512
GDN_TOKEN_TILE = 1024
ATTN_TQ = 512
ATTN_TK = 256
ATTN_PREP_TILES = 4
ATTN_GROUP = 8
ATTN_LOOKAHEAD = 4
SOFTMAX_DENOM_MIN = 2.0 ** -64
SOFTMAX_DENOM_MAX = 2.0 ** 80
FFN_CHUNK = 256
FFN_DOWN_GROUP = 11
FFN_ROW_SPLIT = 2


def _bdot(a, b):
    return jnp.dot(a.astype(bf16), b.astype(bf16), preferred_element_type=f32)


def _bdot_nt(a, b):
    return lax.dot_general(a.astype(bf16), b.astype(bf16), (((1,), (1,)), ((), ())),
                           preferred_element_type=f32)


def _silu(x):
    h = 0.5 * x
    return h + h * jnp.tanh(h)


def _params(semantics):
    return pltpu.CompilerParams(dimension_semantics=semantics, vmem_limit_bytes=VMEM_LIMIT_BYTES)


def _const_spec(shape):
    nd = len(shape)
    return pl.BlockSpec(shape, lambda *_: (0,) * nd, pipeline_mode=pl.Buffered(1))


def _ada_kernel(c_ref, w_ref, b_ref, o_ref):
    ca = _silu(c_ref[...])
    o_ref[0] = _bdot(ca, w_ref[0]) + b_ref[0]


def _ada_modulation(c, w_ada, b_ada):
    L, D, N = w_ada.shape
    B = c.shape[0]
    tn = 1536
    return pl.pallas_call(
        _ada_kernel,
        out_shape=jax.ShapeDtypeStruct((L, B, N), f32),
        grid=(L, N // tn),
        in_specs=[
            pl.BlockSpec((B, D), lambda l, n: (0, 0)),
            pl.BlockSpec((1, D, tn), lambda l, n: (l, 0, n)),
            pl.BlockSpec((1, 1, tn), lambda l, n: (l, 0, n)),
        ],
        out_specs=pl.BlockSpec((1, B, tn), lambda l, n: (l, 0, n)),
        compiler_params=_params(("arbitrary", "arbitrary")),
        name="ada_modulation",
    )(c, w_ada, b_ada.reshape(L, 1, N))


def _bucket_of_distance(n):
    max_exact = REL_BUCKETS // 2
    nf = np.maximum(n, 1).astype(np.float32)
    large = max_exact + (np.log(nf / np.float32(max_exact)) / np.float32(math.log(REL_MAX_DIST / max_exact))
                         * np.float32(REL_BUCKETS - max_exact)).astype(np.int32)
    large = np.minimum(large, REL_BUCKETS - 1)
    return np.where(n < max_exact, n, large).astype(np.int32)


def _bias_bucket_tiles(tq, tk):
    assert tq % tk == 0
    r = tq // tk
    i = np.arange(tq)[None, :]
    j = np.arange(tk)[:, None]
    tiles = []
    for t in range(r + 1):
        dist = i - j + (t - (r - 1)) * tk
        tiles.append(np.where(dist >= 0, _bucket_of_distance(np.maximum(dist, 0)), -1))
    assert _bucket_of_distance(np.array([tk + 1]))[0] == REL_BUCKETS - 1
    near = np.stack(tiles).astype(np.int32)
    return np.concatenate([near, near], axis=2)


def _bias_tile_kernel(rel_ref, bk_ref, o_ref):
    h = pl.program_id(1)
    bk = bk_ref[0]
    last = rel_ref[REL_BUCKETS - 1, h]
    acc = jnp.where(bk < 0, NEG_BIG, 0.0).astype(f32)
    for b in range(REL_BUCKETS - 1):
        acc = jnp.where(bk == b, (rel_ref[b, h] - last) * LOG2E, acc)
    o_ref[0, 0] = acc


def _bias_tiles(rel_bias, tq, tk):
    buckets = jnp.asarray(_bias_bucket_tiles(tq, tk))
    return pl.pallas_call(
        _bias_tile_kernel,
        out_shape=jax.ShapeDtypeStruct((buckets.shape[0], DIFF_HEADS, tk, 2 * tq), f32),
        grid=(buckets.shape[0], DIFF_HEADS),
        in_specs=[
            pl.BlockSpec(memory_space=pltpu.SMEM),
            pl.BlockSpec((1, tk, 2 * tq), lambda k, h: (k, 0, 0)),
        ],
        out_specs=pl.BlockSpec((1, 1, tk, 2 * tq), lambda k, h: (k, h, 0, 0)),
        compiler_params=_params(("arbitrary", "arbitrary")),
        name="rel_bias_tiles",
    )(rel_bias.astype(f32), buckets)


def _modulated_norm(x, gain_row, mod_ref, shift_idx, scale_idx):
    ms = jnp.mean(x * x, axis=-1, keepdims=True)
    y = x * lax.rsqrt(ms + EPS)
    return y * (gain_row * (1.0 + mod_ref[0, scale_idx:scale_idx + 1, :])) + mod_ref[0, shift_idx:shift_idx + 1, :]


def _inproj_kernel(x_ref, mod_ref, g_ref, w_ref, o_ref, h_scr):
    h_scr[...] = _modulated_norm(x_ref[0], g_ref[...], mod_ref, 0, 1).astype(bf16)
    n = o_ref.shape[2]
    for c0 in range(0, n, GROUP_W):
        c1 = min(c0 + GROUP_W, n)
        o_ref[0, :, c0:c1] = jnp.dot(h_scr[...], w_ref[:, c0:c1], preferred_element_type=f32)


def _in_projection(x, mod_l, gain, w_perm):
    B, T, D = x.shape
    tm = TOKEN_TILE
    return pl.pallas_call(
        _inproj_kernel,
        out_shape=jax.ShapeDtypeStruct((B, T, PROJ_W), f32),
        grid=(B, T // tm),
        in_specs=[
            pl.BlockSpec((1, tm, D), lambda b, t: (b, t, 0)),
            pl.BlockSpec((1, 6, D), lambda b, t: (b, 0, 0)),
            _const_spec((1, D)),
            _const_spec((D, PROJ_W)),
        ],
        out_specs=pl.BlockSpec((1, tm, PROJ_W), lambda b, t: (b, t, 0)),
        scratch_shapes=[pltpu.VMEM((tm, D), bf16)],
        compiler_params=_params(("arbitrary", "arbitrary")),
        name="in_projection",
    )(x, mod_l, gain.reshape(1, D), w_perm)


def _shift_rows(x, prev8, s):
    xs = pltpu.roll(x, s, axis=0)
    fix = pltpu.roll(prev8, s, axis=0)
    row = lax.broadcasted_iota(jnp.int32, fix.shape, 0)
    head = jnp.where(row < s, fix, xs[:8])
    return jnp.concatenate([head, xs[8:]], axis=0)


def _causal_conv(x, prev8, w_ref, taps):
    acc = x * w_ref[taps - 1:taps, :]
    for s in range(1, taps):
        acc = acc + _shift_rows(x, prev8, s) * w_ref[taps - 1 - s:taps - s, :]
    return acc


def _expand_block_diagonal(packed):
    c, n = packed.shape
    row = lax.broadcasted_iota(jnp.int32, (n, n), 0)
    col = lax.broadcasted_iota(jnp.int32, (n, n), 1)
    tiled = jnp.concatenate([packed] * (n // c), axis=0)
    return jnp.where((row // c) == (col // c), tiled, 0.0).astype(bf16)


def _packed_lower_inverses(lows, base, out):
    c, n = lows[0].shape
    row = lax.broadcasted_iota(jnp.int32, (c, n), 0)
    col = lax.broadcasted_iota(jnp.int32, (c, n), 1) % c
    same_block = lambda size: (row // size) == (col // size)
    eye = jnp.where(row == col, 1.0, 0.0).astype(f32)
    base_mask = same_block(base)
    lbs = [jnp.where(base_mask, low, 0.0) for low in lows]
    xs = [eye - lb for lb in lbs]
    ps = [jnp.dot(lb.astype(bf16), _expand_block_diagonal(lb), preferred_element_type=f32) for lb in lbs]
    yield
    for _ in range(int(math.log2(base)) - 2):
        xps = [jnp.dot(jnp.concatenate([x, p], axis=0).astype(bf16), _expand_block_diagonal(p),
                       preferred_element_type=f32) for x, p in zip(xs, ps)]
        yield
        xs = [x + xp[:c] for x, xp in zip(xs, xps)]
        ps = [xp[c:] for xp in xps]
    xs = [x + jnp.dot(x.astype(bf16), _expand_block_diagonal(p), preferred_element_type=f32)
          for x, p in zip(xs, ps)]
    yield
    size = base
    while size < c:
        off_mask = jnp.logical_and(same_block(2 * size), jnp.logical_not(same_block(size)))
        ts = [jnp.dot(x.astype(bf16), _expand_block_diagonal(jnp.where(off_mask, low, 0.0)),
                      preferred_element_type=f32) for x, low in zip(xs, lows)]
        yield
        ts = [jnp.dot(t.astype(bf16), _expand_block_diagonal(x), preferred_element_type=f32)
              for t, x in zip(ts, xs)]
        yield
        xs = [x - t for x, t in zip(xs, ts)]
        size *= 2
    out.extend(xs)


def _gdn_kernel(qkv_ref, gate_ref, ba_ref, convw_ref, alog_ref, dt_ref, onorm_ref, o_ref,
                q_s, k_s, v_s, gc_s, beta_s, state_s, carry_s):
    tg = qkv_ref.shape[1]
    C = GDN_CHUNK

    @pl.when(pl.program_id(1) == 0)
    def _():
        state_s[...] = jnp.zeros_like(state_s)
        carry_s[...] = jnp.zeros_like(carry_s)

    def token_prep(r0, nr):
        rows = slice(r0, r0 + nr)
        for grp, dst in enumerate((q_s, k_s, v_s)):
            for h in range(GDN_HEADS):
                c0 = grp * GROUP_W + h * HEAD_W
                cols = slice(c0, c0 + HEAD_W)
                prev8 = carry_s[:, cols] if r0 == 0 else qkv_ref[0, r0 - 8:r0, cols]
                y = _silu(_causal_conv(qkv_ref[0, rows, cols], prev8, convw_ref.at[:, cols], GDN_CONV))
                if grp < 2:
                    y = y * lax.rsqrt(jnp.sum(y * y, axis=-1, keepdims=True) + EPS)
                if grp == 0:
                    y = y * (GDN_DK ** -0.5)
                dst[rows, h * HEAD_W:(h + 1) * HEAD_W] = y
        ba = ba_ref[0, rows, :]
        beta_s[rows, :] = 1.0 / (1.0 + jnp.exp(-ba))
        z = ba + dt_ref[...]
        g = -jnp.exp(alog_ref[...]) * (jnp.maximum(z, 0.0) + jnp.log1p(jnp.exp(-jnp.abs(z))))
        rowc = lax.broadcasted_iota(jnp.int32, g.shape, 0) % C
        s = 1
        while s < C:
            g = g + jnp.where(rowc >= s, pltpu.roll(g, s, axis=0), 0.0)
            s *= 2
        gc_s[rows, :] = g

    H = GDN_HEADS
    n = H * C
    ri = lax.broadcasted_iota(jnp.int32, (C, n), 0)
    ci = lax.broadcasted_iota(jnp.int32, (C, n), 1)
    lane_head = ci // C
    incl = ri >= ci % C
    strict = ri > ci % C
    onorm = onorm_ref[...]

    def pack_diagonal_blocks(big):
        out = big[0:C]
        for h in range(1, H):
            out = jnp.where(lane_head == h, big[h * C:(h + 1) * C], out)
        return out

    def chunk_operands(c):
        rows = slice(c * C, (c + 1) * C)
        gcc = gc_s[rows, :]
        gct = gcc.T
        betac = beta_s[rows, :]
        glast = gc_s[(c + 1) * C - 1:(c + 1) * C, :]
        ks, kbs, qs, rhs, qgs, kdts, gcols, grows, egls = [], [], [], [], [], [], [], [], []
        for h in range(H):
            cols = slice(h * HEAD_W, (h + 1) * HEAD_W)
            q = q_s[rows, cols]
            k = k_s[rows, cols]
            v = v_s[rows, cols]
            beta = betac[:, h:h + 1]
            gcol = gcc[:, H + h:H + h + 1]
            gl = glast[:, H + h:H + h + 1]
            eg = jnp.exp(gcol)
            kb = k * beta
            ks.append(k)
            kbs.append(kb)
            qs.append(q)
            rhs.append(jnp.concatenate([v * beta, kb * eg], axis=1).astype(bf16))
            qgs.append((q * eg).astype(bf16))
            kdts.append((k * jnp.exp(gl - gcol)).T.astype(bf16))
            gcols.append(jnp.broadcast_to(gcol, (C, C)))
            grows.append(gct[H + h:H + h + 1, :])
            egls.append(jnp.exp(gl))
        gcol_all = jnp.concatenate(gcols, axis=1)
        grow_all = jnp.concatenate(grows, axis=1)
        decay = jnp.exp(jnp.where(incl, gcol_all - grow_all, NEG_BIG))
        return dict(kbq=jnp.concatenate(kbs + qs, axis=0).astype(bf16), k_all=jnp.concatenate(ks, axis=0).astype(bf16),
                    rhs=rhs, decay=decay, qg=qgs, kdt=kdts, egl=egls)

    def state_free_part(c0):
        ops = [chunk_operands(c) for c in range(c0, c0 + GDN_LOCKSTEP)]
        if c0 + GDN_LOCKSTEP < tg // C:
            token_prep((c0 + GDN_LOCKSTEP) * C, GDN_LOCKSTEP * C)
        kqs = [lax.dot_general(o["kbq"], o["k_all"], (((1,), (1,)), ((), ())), preferred_element_type=f32)
               for o in ops]
        yield
        lows = [jnp.where(strict, pack_diagonal_blocks(kq[:n]) * o["decay"], 0.0) for kq, o in zip(kqs, ops)]
        for kq, o in zip(kqs, ops):
            o["a"] = (pack_diagonal_blocks(kq[n:]) * o["decay"]).astype(bf16)
        t_invs = []
        yield from _packed_lower_inverses(lows, 16, t_invs)
        for t_inv, o in zip(t_invs, ops):
            t_inv = t_inv.astype(bf16)
            o["uw"] = [jnp.dot(t_inv[:, h * C:(h + 1) * C], o["rhs"][h], preferred_element_type=f32)
                       for h in range(H)]
        yield
        chunks.extend(ops)

    def recurrence(c_first, c_last):
        for c in range(c_first, c_last):
            o = chunks[c]
            rows = slice(c * C, (c + 1) * C)
            uw = o["uw"]
            wss = [jnp.dot(jnp.concatenate([uw[h][:, HEAD_W:].astype(bf16), o["qg"][h]], axis=0),
                           states[h].astype(bf16), preferred_element_type=f32) for h in range(H)]
            yield
            v_news = [uw[h][:, :HEAD_W] - wss[h][:C] for h in range(H)]
            for h in range(H):
                states[h] = (states[h] * o["egl"][h]
                             + jnp.dot(o["kdt"][h], v_news[h].astype(bf16), preferred_element_type=f32))
            yield
            outs = [wss[h][C:] + jnp.dot(o["a"][:, h * C:(h + 1) * C], v_news[h].astype(bf16),
                                         preferred_element_type=f32) for h in range(H)]
            yield
            for h in range(H):
                cols = slice(h * HEAD_W, (h + 1) * HEAD_W)
                oh = outs[h]
                on = oh * lax.rsqrt(jnp.mean(oh * oh, axis=-1, keepdims=True) + EPS) * onorm
                o_ref[0, rows, cols] = (on * _silu(gate_ref[0, rows, cols])).astype(o_ref.dtype)

    chunks = []
    states = [state_s[h] for h in range(H)]
    token_prep(0, GDN_LOCKSTEP * C)
    pending = iter(())
    for c0 in range(0, tg // C, GDN_LOCKSTEP):
        for _ in state_free_part(c0):
            next(pending, None)
        for _ in pending:
            pass
        pending = recurrence(c0, c0 + GDN_LOCKSTEP)
    for _ in pending:
        pass
    for h in range(H):
        state_s[h] = states[h]
    carry_s[...] = qkv_ref[0, tg - 8:tg, :]


def _gdn_mixer(proj, conv_w, a_log, dt_bias, out_norm):
    B, T, _ = proj.shape
    tg = GDN_TOKEN_TILE
    convw =jnp.zeros((8, 3 * GROUP_W), f32).at[:GDN_CONV].set(conv_w.astype(f32))
    lane_pad = lambda v: jnp.zeros((1, HEAD_W), f32).at[0, GDN_HEADS:2 * GDN_HEADS].set(v.astype(f32))
    return pl.pallas_call(
        _gdn_kernel,
        out_shape=jax.ShapeDtypeStruct((B, T, GROUP_W), bf16),
        grid=(B, T // tg),
        in_specs=[
            pl.BlockSpec((1, tg, 3 * GROUP_W), lambda b, t: (b, t, 0)),
            pl.BlockSpec((1, tg, GROUP_W), lambda b, t: (b, t, COL_GATE // GROUP_W)),
            pl.BlockSpec((1, tg, HEAD_W), lambda b, t: (b, t, COL_BA // HEAD_W)),
            _const_spec((8, 3 * GROUP_W)),
            _const_spec((1, HEAD_W)),
            _const_spec((1, HEAD_W)),
            _const_spec((1, HEAD_W)),
        ],
        out_specs=pl.BlockSpec((1, tg, GROUP_W), lambda b, t: (b, t, 0)),
        scratch_shapes=[
            pltpu.VMEM((tg, GROUP_W), f32),
            pltpu.VMEM((tg, GROUP_W), f32),
            pltpu.VMEM((tg, GROUP_W), f32),
            pltpu.VMEM((tg, HEAD_W), f32),
            pltpu.VMEM((tg, HEAD_W), f32),
            pltpu.VMEM((GDN_HEADS, GDN_DK, HEAD_W), f32),
            pltpu.VMEM((8, 3 * GROUP_W), f32),
        ],
        compiler_params=_params(("arbitrary", "arbitrary")),
        name="gdn_mixer",
    )(proj, proj, proj, convw, lane_pad(a_log), lane_pad(dt_bias), out_norm.reshape(1, HEAD_W).astype(f32))


def _half_rmsnorm(x, gain_row):
    w = x.shape[1]
    ri = lax.broadcasted_iota(jnp.int32, (w, w), 0)
    ci = lax.broadcasted_iota(jnp.int32, (w, w), 1)
    same_half = jnp.where((ri // DIFF_DH) == (ci // DIFF_DH), 1.0, 0.0).astype(bf16)
    x2 = x * x
    hi = x2.astype(bf16)
    lo = (x2 - hi.astype(f32)).astype(bf16)
    ss = (jnp.dot(hi, same_half, preferred_element_type=f32)
          + jnp.dot(lo, same_half, preferred_element_type=f32))
    return x * lax.rsqrt(ss * (1.0 / DIFF_DH) + EPS) * gain_row


def _attn_kernel(q_ref, k_ref, v_ref, db_ref, qn_ref, kn_ref, lam_ref, subln_ref, o_ref,
                 k_s, vt_s, q_s, m_s, acc_s, *, lambda_init):
    tq = q_ref.shape[1]
    tk = db_ref.shape[2]
    T = k_ref.shape[1]
    dv = v_ref.shape[2]
    qi = pl.program_id(2)

    @pl.when(qi == 0)
    def _():
        ones_rows = jnp.where(lax.broadcasted_iota(jnp.int32, (vt_s.shape[1] - dv, tk), 0) == 0, 1.0, 0.0)

        def prep(i, carry):
            tiles = [i * ATTN_PREP_TILES + u for u in range(ATTN_PREP_TILES)]
            rows = [pl.ds(pl.multiple_of(j * tk, tk), tk) for j in tiles]
            kn = [_half_rmsnorm(k_ref[0, r, :], kn_ref[...]) for r in rows]
            vt = [v_ref[0, r, :].T for r in rows]
            for j, k, v in zip(tiles, kn, vt):
                k_s[j] = k.astype(bf16)
                vt_s[j] = jnp.concatenate([v, ones_rows], axis=0).astype(bf16)
            return carry
        lax.fori_loop(0, T // (tk * ATTN_PREP_TILES), prep, 0)

    qt = (_half_rmsnorm(q_ref[0], qn_ref[...]) * (DIFF_DH ** -0.5 * LOG2E)).T
    dim = lax.broadcasted_iota(jnp.int32, qt.shape, 0)
    q_s[0] = jnp.where(dim < DIFF_DH, qt, 0.0).astype(bf16)
    q_s[1] = jnp.where(dim < DIFF_DH, 0.0, qt).astype(bf16)

    r = tq // tk
    n_near = db_ref.shape[0]
    near_tiles = [r * (qi + 1) - 1 - t for t in range(n_near)]
    near_bias = [db_ref[t, 0] for t in range(n_near)]
    n_far = jnp.maximum(r * qi - 1, 0)
    n_grouped = (n_far // ATTN_GROUP) * ATTN_GROUP

    def unshifted(js, biases, acc):
        units = [(t, c) for t in range(len(js)) for c in range(2)]
        scores = {}
        for i in range(len(units) + ATTN_LOOKAHEAD):
            if i < len(units):
                t, c = units[i]
                s = jnp.dot(k_s[js[t]], q_s[c], preferred_element_type=f32)
                scores[i] = s if biases[t] is None else s + biases[t][:, c * tq:(c + 1) * tq]
            if i >= ATTN_LOOKAHEAD:
                t, c = units[i - ATTN_LOOKAHEAD]
                pv = jnp.dot(vt_s[js[t]], jnp.exp2(scores.pop(i - ATTN_LOOKAHEAD)).astype(bf16),
                             preferred_element_type=f32)
                acc[c] = pv if acc[c] is None else acc[c] + pv
        return acc

    @pl.when(qi == 0)
    def _():
        acc = unshifted(near_tiles[:r], near_bias[:r], [None, None])
        for c in range(2):
            acc_s[c] = acc[c]

    for rem in range(ATTN_GROUP):
        @pl.when(jnp.logical_and(qi >= 1, n_far - n_grouped == rem))
        def _():
            acc = unshifted(near_tiles + [n_grouped + t for t in range(rem)],
                            near_bias + [None] * rem, [None, None])
            for c in range(2):
                acc_s[c] = acc[c]

    def far_group(i, carry):
        acc = unshifted([ATTN_GROUP * i + t for t in range(ATTN_GROUP)], [None] * ATTN_GROUP,
                        [acc_s[0], acc_s[1]])
        for c in range(2):
            acc_s[c] = acc[c]
        return carry
    lax.fori_loop(0, n_far // ATTN_GROUP, far_group, 0)

    denom = jnp.concatenate([acc_s[0, dv:dv + 1, :], acc_s[1, dv:dv + 1, :]], axis=1)
    trusted = jnp.logical_and(jnp.min(denom) >= SOFTMAX_DENOM_MIN, jnp.max(denom) <= SOFTMAX_DENOM_MAX)

    @pl.when(jnp.logical_not(trusted))
    def _():
        m_s[...] = jnp.full_like(m_s, NEG_BIG)
        acc_s[...] = jnp.zeros_like(acc_s)

        def online(js, biases):
            for c in range(2):
                ss = []
                for j, bias in zip(js, biases):
                    s = jnp.dot(k_s[j], q_s[c], preferred_element_type=f32)
                    ss.append(s if bias is None else s + bias[:, c * tq:(c + 1) * tq])
                m_old = m_s[c]
                m_new = m_old
                for s in ss:
                    m_new = jnp.maximum(m_new, jnp.max(s, axis=0, keepdims=True))
                acc = jnp.exp2(m_old - m_new) * acc_s[c]
                for j, s in zip(js, ss):
                    acc = acc + jnp.dot(vt_s[j], jnp.exp2(s - m_new).astype(bf16), preferred_element_type=f32)
                acc_s[c] = acc
                m_s[c] = m_new

        for t in range(r):
            online([near_tiles[t]], [near_bias[t]])

        @pl.when(qi >= 1)
        def _():
            online([near_tiles[r]], [near_bias[r]])

        def far_tile(j, carry):
            online([j], [None])
            return carry
        lax.fori_loop(0, n_far, far_tile, 0)

    lam_rows = lam_ref[...]
    lam = (jnp.exp(jnp.sum(lam_rows[0:1] * lam_rows[1:2], axis=-1, keepdims=True))
           - jnp.exp(jnp.sum(lam_rows[2:3] * lam_rows[3:4], axis=-1, keepdims=True)) + lambda_init)
    acc1 = acc_s[0]
    acc2 = acc_s[1]
    ot = acc1[:dv] * (1.0 / acc1[dv:dv + 1]) - lam * (acc2[:dv] * (1.0 / acc2[dv:dv + 1]))
    ont = ot * lax.rsqrt(jnp.mean(ot * ot, axis=0, keepdims=True) + EPS)
    o_ref[0] = (ont.T * (subln_ref[...] * (1.0 - lambda_init))).astype(o_ref.dtype)


def _diff_attention(proj, db, q_norm, k_norm, lq1, lk1, lq2, lk2, subln, lambda_init):
    B, T, _ = proj.shape
    tq, tk = ATTN_TQ, ATTN_TK
    two = lambda v: jnp.concatenate([v, v]).reshape(1, HEAD_W).astype(f32)
    lam_rows = jnp.zeros((8, DIFF_DH), f32).at[:4].set(jnp.stack([lq1, lk1, lq2, lk2]).astype(f32))
    qb, kb, vb = COL_DQ // HEAD_W, COL_DK // HEAD_W, COL_DV // HEAD_W
    return pl.pallas_call(
        functools.partial(_attn_kernel, lambda_init=lambda_init),
        out_shape=jax.ShapeDtypeStruct((B, T, GROUP_W), bf16),
        grid=(B, DIFF_HEADS, T // tq),
        in_specs=[
            pl.BlockSpec((1, tq, HEAD_W), lambda b, h, i: (b, i, qb + h)),
            pl.BlockSpec((1, T, HEAD_W), lambda b, h, i: (b, 0, kb + h)),
            pl.BlockSpec((1, T, HEAD_W), lambda b, h, i: (b, 0, vb + h)),
            pl.BlockSpec((tq // tk + 1, 1, tk, 2 * tq), lambda b, h, i: (0, h, 0, 0)),
            _const_spec((1, HEAD_W)),
            _const_spec((1, HEAD_W)),
            _const_spec((8, DIFF_DH)),
            _const_spec((1, HEAD_W)),
        ],
        out_specs=pl.BlockSpec((1, tq, HEAD_W), lambda b, h, i: (b, i, h)),
        scratch_shapes=[
            pltpu.VMEM((T // tk, tk, HEAD_W), bf16),
            pltpu.VMEM((T // tk, HEAD_W + BF16_ROWS, tk), bf16),
            pltpu.VMEM((2, HEAD_W, tq), bf16),
            pltpu.VMEM((2, 1, tq), f32),
            pltpu.VMEM((2, HEAD_W + BF16_ROWS, tq), f32),
        ],
        compiler_params=_params(("arbitrary", "arbitrary", "arbitrary")),
        name="diff_attention",
    )(proj, proj, proj, db, two(q_norm), two(k_norm), lam_rows, subln.reshape(1, HEAD_W).astype(f32))


def _outffn_kernel(x_ref, yg_ref, yd_ref, mod_ref, g_ref, wout_ref, wup_ref, cw_ref, wdown_ref, o_ref,
                   h_s, acc_s, carry_s):
    nchunks = wup_ref.shape[0]
    fc = wdown_ref.shape[1]

    @pl.when(pl.program_id(1) == 0)
    def _():
        carry_s[...] = jnp.zeros_like(carry_s)

    half = yg_ref.shape[2]
    tm = h_s.shape[0]
    tr = tm // FFN_ROW_SPLIT
    acc_s[...] = jnp.zeros_like(acc_s)

    ys = [jnp.dot(yg_ref[0, r0:r0 + tr, :], wout_ref[:half, :], preferred_element_type=f32)
          + jnp.dot(yd_ref[0, r0:r0 + tr, :], wout_ref[half:, :], preferred_element_type=f32)
          for r0 in range(0, tm, tr)]
    us = []
    for r0, y in zip(range(0, tm, tr), ys):
        x1 = x_ref[0, r0:r0 + tr, :] + mod_ref[0, 2:3, :] * y
        o_ref[0, r0:r0 + tr, :] = x1
        h_s[r0:r0 + tr, :] = _modulated_norm(x1, g_ref[...], mod_ref, 3, 4).astype(bf16)
        us.append(jnp.dot(h_s[r0:r0 + tr, :], wup_ref[0], preferred_element_type=f32))

    def up(j):
        return [jnp.dot(h_s[r0:r0 + tr, :], wup_ref[j], preferred_element_type=f32)
                for r0 in range(0, tm, tr)]

    held = [[] for _ in range(FFN_ROW_SPLIT)]
    for j in range(nchunks):
        us_next = up(j + 1) if j + 1 < nchunks else None
        cw = cw_ref.at[j]
        prev8 = carry_s[j]
        flush = (j + 1) % FFN_DOWN_GROUP == 0 or j + 1 == nchunks
        for i, u in enumerate(us):
            yc = _causal_conv(u, prev8, cw, FFN_CONV) + cw[FFN_CONV:FFN_CONV + 1, :]
            prev8 = u[tr - 8:, :]
            held[i].append((_silu(yc[:, fc:]) * yc[:, :fc]).astype(bf16))
            if flush:
                j0 = j + 1 - len(held[i])
                w_down = jnp.concatenate([wdown_ref[jj] for jj in range(j0, j + 1)], axis=0)
                acc_s[i * tr:(i + 1) * tr, :] += jnp.dot(jnp.concatenate(held[i], axis=1), w_down,
                                                         preferred_element_type=f32)
                held[i] = []
        carry_s[j] = prev8
        us = us_next
    o_ref[0] = o_ref[0] + mod_ref[0, 5:6, :] * acc_s[...]


def _out_ffn(x, y_gdn, y_diff, mod_l, gain, w_out, w_up_c, conv_c, w_down_c):
    B, T, D = x.shape
    tm = TOKEN_TILE
    nchunks, _, fc2 = w_up_c.shape
    fc = fc2 // 2
    return pl.pallas_call(
        _outffn_kernel,
        out_shape=jax.ShapeDtypeStruct((B, T, D), f32),
        grid=(B, T // tm),
        in_specs=[
            pl.BlockSpec((1, tm, D), lambda b, t: (b, t, 0)),
            pl.BlockSpec((1, tm, GROUP_W), lambda b, t: (b, t, 0)),
            pl.BlockSpec((1, tm, GROUP_W), lambda b, t: (b, t, 0)),
            pl.BlockSpec((1, 6, D), lambda b, t: (b, 0, 0)),
            _const_spec((1, D)),
            _const_spec((2 * GROUP_W, D)),
            _const_spec((nchunks, D, fc2)),
            _const_spec((nchunks, 8, fc2)),
            _const_spec((nchunks, fc, D)),
        ],
        out_specs=pl.BlockSpec((1, tm, D), lambda b, t: (b, t, 0)),
        scratch_shapes=[
            pltpu.VMEM((tm, D), bf16),
            pltpu.VMEM((tm, D), f32),
            pltpu.VMEM((nchunks, 8, fc2), f32),
        ],
        compiler_params=_params(("arbitrary", "arbitrary")),
        name="out_ffn",
    )(x, y_gdn, y_diff, mod_l, gain.reshape(1, D), w_out, w_up_c, conv_c, w_down_c)


def _regroup_w_in(w_in_l):
    D = w_in_l.shape[0]
    o_b = 4 * GROUP_W
    o_d = o_b + 2 * GDN_HEADS
    small = jnp.zeros((D, HEAD_W), w_in_l.dtype).at[:, :2 * GDN_HEADS].set(w_in_l[:, o_b:o_d])
    return jnp.concatenate([w_in_l[:, :o_b], w_in_l[:, o_d:], small], axis=1).astype(bf16)


def _chunk_ffn(ffn_up_l, conv_w_l, conv_b_l, ffn_down_l, fc):
    D, two_f = ffn_up_l.shape
    F = two_f // 2
    n = F // fc
    pair = lambda m: jnp.concatenate([m[..., :F].reshape(m.shape[:-1] + (n, fc)),
                                      m[..., F:].reshape(m.shape[:-1] + (n, fc))], axis=-1)
    w_up_c = jnp.moveaxis(pair(ffn_up_l), -2, 0).astype(bf16)
    conv_rows = jnp.concatenate([conv_w_l.astype(f32), conv_b_l.astype(f32)[None]], axis=0)
    conv_c = jnp.moveaxis(pair(conv_rows), -2, 0)
    conv_c = jnp.zeros((n, 8, 2 * fc), f32).at[:, :FFN_CONV + 1].set(conv_c)
    w_down_c = ffn_down_l.reshape(n, fc, ffn_down_l.shape[1]).astype(bf16)
    return w_up_c, conv_c, w_down_c


def kernel(x, c, w_ada, b_ada, norm_mix, norm_ffn, w_in, gdn_conv_w, gdn_a_log, gdn_dt_bias, gdn_out_norm,
           diff_q_norm, diff_k_norm, diff_lambda_q1, diff_lambda_k1, diff_lambda_q2, diff_lambda_k2, diff_subln,
           rel_bias, w_out, ffn_up, ffn_conv_w, ffn_conv_b, ffn_down):
    B, T, D = x.shape
    depth = w_in.shape[0]
    mod = _ada_modulation(c, w_ada, b_ada).reshape(depth, B, 6, D)
    db = _bias_tiles(rel_bias, ATTN_TQ, ATTN_TK)
    for l in range(depth):
        proj = _in_projection(x, mod[l], norm_mix[l], _regroup_w_in(w_in[l]))
        y_gdn = _gdn_mixer(proj, gdn_conv_w[l], gdn_a_log[l], gdn_dt_bias[l], gdn_out_norm[l])
        lambda_init = 0.8 - 0.6 * math.exp(-0.3 * l)
        y_diff = _diff_attention(proj, db, diff_q_norm[l], diff_k_norm[l], diff_lambda_q1[l], diff_lambda_k1[l],
                                 diff_lambda_q2[l], diff_lambda_k2[l], diff_subln[l], lambda_init)
        w_up_c, conv_c, w_down_c = _chunk_ffn(ffn_up[l], ffn_conv_w[l], ffn_conv_b[l], ffn_down[l], FFN_CHUNK)
        x = _out_ffn(x, y_gdn, y_diff, mod[l], norm_ffn[l], w_out[l].astype(bf16), w_up_c, conv_c, w_down_c)
    return x
```

```python
import functools
import math

import numpy as np
import jax
import jax.numpy as jnp
from jax import lax
from jax.experimental import pallas as pl
from jax.experimental.pallas import tpu as pltpu

f32 = jnp.float32
bf16 = jnp.bfloat16

EPS = 1e-6
GDN_HEADS = 4
GDN_DK = 128
GDN_CONV = 4
GDN_CHUNK = 64
GDN_LOCKSTEP = 4
DIFF_HEADS = 4
DIFF_DH = 64
REL_BUCKETS = 32
REL_MAX_DIST = 128
FFN_CONV = 3
HEAD_W = 128
GROUP_W = GDN_HEADS * HEAD_W
NEG_BIG = -1e30
LOG2E = math.log2(math.e)
BF16_ROWS = 16

VMEM_LIMIT_BYTES = 56 * 1024 * 1024

COL_GQ, COL_GK, COL_GV, COL_GATE, COL_DQ, COL_DK, COL_DV, COL_BA = (
    0, GROUP_W, 2 * GROUP_W, 3 * GROUP_W, 4 * GROUP_W, 5 * GROUP_W, 6 * GROUP_W, 7 * GROUP_W)
PROJ_W = 7 * GROUP_W + HEAD_W

TOKEN_TILE = 512
GDN_TOKEN_TILE = 1024
ATTN_TQ = 512
ATTN_TK = 256
ATTN_PREP_TILES = 4
ATTN_GROUP = 8
ATTN_LOOKAHEAD = 4
SOFTMAX_DENOM_MIN = 2.0 ** -64
SOFTMAX_DENOM_MAX = 2.0 ** 80
FFN_CHUNK = 256
FFN_DOWN_GROUP = 11
FFN_ROW_SPLIT = 2


def _bdot(a, b):
    return jnp.dot(a.astype(bf16), b.astype(bf16), preferred_element_type=f32)


def _bdot_nt(a, b):
    return lax.dot_general(a.astype(bf16), b.astype(bf16), (((1,), (1,)), ((), ())),
                           preferred_element_type=f32)


def _silu(x):
    h = 0.5 * x
    return h + h * jnp.tanh(h)


def _params(semantics):
    return pltpu.CompilerParams(dimension_semantics=semantics, vmem_limit_bytes=VMEM_LIMIT_BYTES)


def _const_spec(shape):
    nd = len(shape)
    return pl.BlockSpec(shape, lambda *_: (0,) * nd, pipeline_mode=pl.Buffered(1))


def _ada_kernel(c_ref, w_ref, b_ref, o_ref):
    ca = _silu(c_ref[...])
    o_ref[0] = _bdot(ca, w_ref[0]) + b_ref[0]


def _ada_modulation(c, w_ada, b_ada):
    L, D, N = w_ada.shape
    B = c.shape[0]
    tn = 1536
    return pl.pallas_call(
        _ada_kernel,
        out_shape=jax.ShapeDtypeStruct((L, B, N), f32),
        grid=(L, N // tn),
        in_specs=[
            pl.BlockSpec((B, D), lambda l, n: (0, 0)),
            pl.BlockSpec((1, D, tn), lambda l, n: (l, 0, n)),
            pl.BlockSpec((1, 1, tn), lambda l, n: (l, 0, n)),
        ],
        out_specs=pl.BlockSpec((1, B, tn), lambda l, n: (l, 0, n)),
        compiler_params=_params(("arbitrary", "arbitrary")),
        name="ada_modulation",
    )(c, w_ada, b_ada.reshape(L, 1, N))


def _bucket_of_distance(n):
    max_exact = REL_BUCKETS // 2
    nf = np.maximum(n, 1).astype(np.float32)
    large = max_exact + (np.log(nf / np.float32(max_exact)) / np.float32(math.log(REL_MAX_DIST / max_exact))
                         * np.float32(REL_BUCKETS - max_exact)).astype(np.int32)
    large = np.minimum(large, REL_BUCKETS - 1)
    return np.where(n < max_exact, n, large).astype(np.int32)


def _bias_bucket_tiles(tq, tk):
    assert tq % tk == 0
    r = tq // tk
    i = np.arange(tq)[None, :]
    j = np.arange(tk)[:, None]
    tiles = []
    for t in range(r + 1):
        dist = i - j + (t - (r - 1)) * tk
        tiles.append(np.where(dist >= 0, _bucket_of_distance(np.maximum(dist, 0)), -1))
    assert _bucket_of_distance(np.array([tk + 1]))[0] == REL_BUCKETS - 1
    near = np.stack(tiles).astype(np.int32)
    return np.concatenate([near, near], axis=2)


def _bias_tile_kernel(rel_ref, bk_ref, o_ref):
    h = pl.program_id(1)
    bk = bk_ref[0]
    last = rel_ref[REL_BUCKETS - 1, h]
    acc = jnp.where(bk < 0, NEG_BIG, 0.0).astype(f32)
    for b in range(REL_BUCKETS - 1):
        acc = jnp.where(bk == b, (rel_ref[b, h] - last) * LOG2E, acc)
    o_ref[0, 0] = acc


def _bias_tiles(rel_bias, tq, tk):
    buckets = jnp.asarray(_bias_bucket_tiles(tq, tk))
    return pl.pallas_call(
        _bias_tile_kernel,
        out_shape=jax.ShapeDtypeStruct((buckets.shape[0], DIFF_HEADS, tk, 2 * tq), f32),
        grid=(buckets.shape[0], DIFF_HEADS),
        in_specs=[
            pl.BlockSpec(memory_space=pltpu.SMEM),
            pl.BlockSpec((1, tk, 2 * tq), lambda k, h: (k, 0, 0)),
        ],
        out_specs=pl.BlockSpec((1, 1, tk, 2 * tq), lambda k, h: (k, h, 0, 0)),
        compiler_params=_params(("arbitrary", "arbitrary")),
        name="rel_bias_tiles",
    )(rel_bias.astype(f32), buckets)


def _modulated_norm(x, gain_row, mod_ref, shift_idx, scale_idx):
    ms = jnp.mean(x * x, axis=-1, keepdims=True)
    y = x * lax.rsqrt(ms + EPS)
    return y * (gain_row * (1.0 + mod_ref[0, scale_idx:scale_idx + 1, :])) + mod_ref[0, shift_idx:shift_idx + 1, :]


def _inproj_kernel(x_ref, mod_ref, g_ref, w_ref, o_ref, h_scr):
    h_scr[...] = _modulated_norm(x_ref[0], g_ref[...], mod_ref, 0, 1).astype(bf16)
    n = o_ref.shape[2]
    for c0 in range(0, n, GROUP_W):
        c1 = min(c0 + GROUP_W, n)
        o_ref[0, :, c0:c1] = jnp.dot(h_scr[...], w_ref[:, c0:c1], preferred_element_type=f32)


def _in_projection(x, mod_l, gain, w_perm):
    B, T, D = x.shape
    tm = TOKEN_TILE
    return pl.pallas_call(
        _inproj_kernel,
        out_shape=jax.ShapeDtypeStruct((B, T, PROJ_W), f32),
        grid=(B, T // tm),
        in_specs=[
            pl.BlockSpec((1, tm, D), lambda b, t: (b, t, 0)),
            pl.BlockSpec((1, 6, D), lambda b, t: (b, 0, 0)),
            _const_spec((1, D)),
            _const_spec((D, PROJ_W)),
        ],
        out_specs=pl.BlockSpec((1, tm, PROJ_W), lambda b, t: (b, t, 0)),
        scratch_shapes=[pltpu.VMEM((tm, D), bf16)],
        compiler_params=_params(("arbitrary", "arbitrary")),
        name="in_projection",
    )(x, mod_l, gain.reshape(1, D), w_perm)


def _shift_rows(x, prev8, s):
    xs = pltpu.roll(x, s, axis=0)
    fix = pltpu.roll(prev8, s, axis=0)
    row = lax.broadcasted_iota(jnp.int32, fix.shape, 0)
    head = jnp.where(row < s, fix, xs[:8])
    return jnp.concatenate([head, xs[8:]], axis=0)


def _causal_conv(x, prev8, w_ref, taps):
    acc = x * w_ref[taps - 1:taps, :]
    for s in range(1, taps):
        acc = acc + _shift_rows(x, prev8, s) * w_ref[taps - 1 - s:taps - s, :]
    return acc


def _expand_block_diagonal(packed):
    c, n = packed.shape
    row = lax.broadcasted_iota(jnp.int32, (n, n), 0)
    col = lax.broadcasted_iota(jnp.int32, (n, n), 1)
    tiled = jnp.concatenate([packed] * (n // c), axis=0)
    return jnp.where((row // c) == (col // c), tiled, 0.0).astype(bf16)


def _packed_lower_inverses(lows, base, out):
    c, n = lows[0].shape
    row = lax.broadcasted_iota(jnp.int32, (c, n), 0)
    col = lax.broadcasted_iota(jnp.int32, (c, n), 1) % c
    same_block = lambda size: (row // size) == (col // size)
    eye = jnp.where(row == col, 1.0, 0.0).astype(f32)
    base_mask = same_block(base)
    lbs = [jnp.where(base_mask, low, 0.0) for low in lows]
    xs = [eye - lb for lb in lbs]
    ps = [jnp.dot(lb.astype(bf16), _expand_block_diagonal(lb), preferred_element_type=f32) for lb in lbs]
    yield
    for _ in range(int(math.log2(base)) - 2):
        xps = [jnp.dot(jnp.concatenate([x, p], axis=0).astype(bf16), _expand_block_diagonal(p),
                       preferred_element_type=f32) for x, p in zip(xs, ps)]
        yield
        xs = [x + xp[:c] for x, xp in zip(xs, xps)]
        ps = [xp[c:] for xp in xps]
    xs = [x + jnp.dot(x.astype(bf16), _expand_block_diagonal(p), preferred_element_type=f32)
          for x, p in zip(xs, ps)]
    yield
    size = base
    while size < c:
        off_mask = jnp.logical_and(same_block(2 * size), jnp.logical_not(same_block(size)))
        ts = [jnp.dot(x.astype(bf16), _expand_block_diagonal(jnp.where(off_mask, low, 0.0)),
                      preferred_element_type=f32) for x, low in zip(xs, lows)]
        yield
        ts = [jnp.dot(t.astype(bf16), _expand_block_diagonal(x), preferred_element_type=f32)
              for t, x in zip(ts, xs)]
        yield
        xs = [x - t for x, t in zip(xs, ts)]
        size *= 2
    out.extend(xs)


def _gdn_kernel(qkv_ref, gate_ref, ba_ref, convw_ref, alog_ref, dt_ref, onorm_ref, o_ref,
                q_s, k_s, v_s, gc_s, beta_s, state_s, carry_s):
    tg = qkv_ref.shape[1]
    C = GDN_CHUNK

    @pl.when(pl.program_id(1) == 0)
    def _():
        state_s[...] = jnp.zeros_like(state_s)
        carry_s[...] = jnp.zeros_like(carry_s)

    def token_prep(r0, nr):
        rows = slice(r0, r0 + nr)
        for grp, dst in enumerate((q_s, k_s, v_s)):
            for h in range(GDN_HEADS):
                c0 = grp * GROUP_W + h * HEAD_W
                cols = slice(c0, c0 + HEAD_W)
                prev8 = carry_s[:, cols] if r0 == 0 else qkv_ref[0, r0 - 8:r0, cols]
                y = _silu(_causal_conv(qkv_ref[0, rows, cols], prev8, convw_ref.at[:, cols], GDN_CONV))
                if grp < 2:
                    y = y * lax.rsqrt(jnp.sum(y * y, axis=-1, keepdims=True) + EPS)
                if grp == 0:
                    y = y * (GDN_DK ** -0.5)
                dst[rows, h * HEAD_W:(h + 1) * HEAD_W] = y
        ba = ba_ref[0, rows, :]
        beta_s[rows, :] = 1.0 / (1.0 + jnp.exp(-ba))
        z = ba + dt_ref[...]
        g = -jnp.exp(alog_ref[...]) * (jnp.maximum(z, 0.0) + jnp.log1p(jnp.exp(-jnp.abs(z))))
        rowc = lax.broadcasted_iota(jnp.int32, g.shape, 0) % C
        s = 1
        while s < C:
            g = g + jnp.where(rowc >= s, pltpu.roll(g, s, axis=0), 0.0)
            s *= 2
        gc_s[rows, :] = g

    H = GDN_HEADS
    n = H * C
    ri = lax.broadcasted_iota(jnp.int32, (C, n), 0)
    ci = lax.broadcasted_iota(jnp.int32, (C, n), 1)
    lane_head = ci // C
    incl = ri >= ci % C
    strict = ri > ci % C
    onorm = onorm_ref[...]

    def pack_diagonal_blocks(big):
        out = big[0:C]
        for h in range(1, H):
            out = jnp.where(lane_head == h, big[h * C:(h + 1) * C], out)
        return out

    def chunk_operands(c):
        rows = slice(c * C, (c + 1) * C)
        gcc = gc_s[rows, :]
        gct = gcc.T
        betac = beta_s[rows, :]
        glast = gc_s[(c + 1) * C - 1:(c + 1) * C, :]
        ks, kbs, qs, rhs, qgs, kdts, gcols, grows, egls = [], [], [], [], [], [], [], [], []
        for h in range(H):
            cols = slice(h * HEAD_W, (h + 1) * HEAD_W)
            q = q_s[rows, cols]
            k = k_s[rows, cols]
            v = v_s[rows, cols]
            beta = betac[:, h:h + 1]
            gcol = gcc[:, H + h:H + h + 1]
            gl = glast[:, H + h:H + h + 1]
            eg = jnp.exp(gcol)
            kb = k * beta
            ks.append(k)
            kbs.append(kb)
            qs.append(q)
            rhs.append(jnp.concatenate([v * beta, kb * eg], axis=1).astype(bf16))
            qgs.append((q * eg).astype(bf16))
            kdts.append((k * jnp.exp(gl - gcol)).T.astype(bf16))
            gcols.append(jnp.broadcast_to(gcol, (C, C)))
            grows.append(gct[H + h:H + h + 1, :])
            egls.append(jnp.exp(gl))
        gcol_all = jnp.concatenate(gcols, axis=1)
        grow_all = jnp.concatenate(grows, axis=1)
        decay = jnp.exp(jnp.where(incl, gcol_all - grow_all, NEG_BIG))
        return dict(kbq=jnp.concatenate(kbs + qs, axis=0).astype(bf16), k_all=jnp.concatenate(ks, axis=0).astype(bf16),
                    rhs=rhs, decay=decay, qg=qgs, kdt=kdts, egl=egls)

    def state_free_part(c0):
        ops = [chunk_operands(c) for c in range(c0, c0 + GDN_LOCKSTEP)]
        if c0 + GDN_LOCKSTEP < tg // C:
            token_prep((c0 + GDN_LOCKSTEP) * C, GDN_LOCKSTEP * C)
        kqs = [lax.dot_general(o["kbq"], o["k_all"], (((1,), (1,)), ((), ())), preferred_element_type=f32)
               for o in ops]
        yield
        lows = [jnp.where(strict, pack_diagonal_blocks(kq[:n]) * o["decay"], 0.0) for kq, o in zip(kqs, ops)]
        for kq, o in zip(kqs, ops):
            o["a"] = (pack_diagonal_blocks(kq[n:]) * o["decay"]).astype(bf16)
        t_invs = []
        yield from _packed_lower_inverses(lows, 16, t_invs)
        for t_inv, o in zip(t_invs, ops):
            t_inv = t_inv.astype(bf16)
            o["uw"] = [jnp.dot(t_inv[:, h * C:(h + 1) * C], o["rhs"][h], preferred_element_type=f32)
                       for h in range(H)]
        yield
        chunks.extend(ops)

    def recurrence(c_first, c_last):
        for c in range(c_first, c_last):
            o = chunks[c]
            rows = slice(c * C, (c + 1) * C)
            uw = o["uw"]
            wss = [jnp.dot(jnp.concatenate([uw[h][:, HEAD_W:].astype(bf16), o["qg"][h]], axis=0),
                           states[h].astype(bf16), preferred_element_type=f32) for h in range(H)]
            yield
            v_news = [uw[h][:, :HEAD_W] - wss[h][:C] for h in range(H)]
            for h in range(H):
                states[h] = (states[h] * o["egl"][h]
                             + jnp.dot(o["kdt"][h], v_news[h].astype(bf16), preferred_element_type=f32))
            yield
            outs = [wss[h][C:] + jnp.dot(o["a"][:, h * C:(h + 1) * C], v_news[h].astype(bf16),
                                         preferred_element_type=f32) for h in range(H)]
            yield
            for h in range(H):
                cols = slice(h * HEAD_W, (h + 1) * HEAD_W)
                oh = outs[h]
                on = oh * lax.rsqrt(jnp.mean(oh * oh, axis=-1, keepdims=True) + EPS) * onorm
                o_ref[0, rows, cols] = (on * _silu(gate_ref[0, rows, cols])).astype(o_ref.dtype)

    chunks = []
    states = [state_s[h] for h in range(H)]
    token_prep(0, GDN_LOCKSTEP * C)
    pending = iter(())
    for c0 in range(0, tg // C, GDN_LOCKSTEP):
        for _ in state_free_part(c0):
            next(pending, None)
        for _ in pending:
            pass
        pending = recurrence(c0, c0 + GDN_LOCKSTEP)
    for _ in pending:
        pass
    for h in range(H):
        state_s[h] = states[h]
    carry_s[...] = qkv_ref[0, tg - 8:tg, :]


def _gdn_mixer(proj, conv_w, a_log, dt_bias, out_norm):
    B, T, _ = proj.shape
    tg = GDN_TOKEN_TILE
    convw =jnp.zeros((8, 3 * GROUP_W), f32).at[:GDN_CONV].set(conv_w.astype(f32))
    lane_pad = lambda v: jnp.zeros((1, HEAD_W), f32).at[0, GDN_HEADS:2 * GDN_HEADS].set(v.astype(f32))
    return pl.pallas_call(
        _gdn_kernel,
        out_shape=jax.ShapeDtypeStruct((B, T, GROUP_W), bf16),
        grid=(B, T // tg),
        in_specs=[
            pl.BlockSpec((1, tg, 3 * GROUP_W), lambda b, t: (b, t, 0)),
            pl.BlockSpec((1, tg, GROUP_W), lambda b, t: (b, t, COL_GATE // GROUP_W)),
            pl.BlockSpec((1, tg, HEAD_W), lambda b, t: (b, t, COL_BA // HEAD_W)),
            _const_spec((8, 3 * GROUP_W)),
            _const_spec((1, HEAD_W)),
            _const_spec((1, HEAD_W)),
            _const_spec((1, HEAD_W)),
        ],
        out_specs=pl.BlockSpec((1, tg, GROUP_W), lambda b, t: (b, t, 0)),
        scratch_shapes=[
            pltpu.VMEM((tg, GROUP_W), f32),
            pltpu.VMEM((tg, GROUP_W), f32),
            pltpu.VMEM((tg, GROUP_W), f32),
            pltpu.VMEM((tg, HEAD_W), f32),
            pltpu.VMEM((tg, HEAD_W), f32),
            pltpu.VMEM((GDN_HEADS, GDN_DK, HEAD_W), f32),
            pltpu.VMEM((8, 3 * GROUP_W), f32),
        ],
        compiler_params=_params(("arbitrary", "arbitrary")),
        name="gdn_mixer",
    )(proj, proj, proj, convw, lane_pad(a_log), lane_pad(dt_bias), out_norm.reshape(1, HEAD_W).astype(f32))


def _half_rmsnorm(x, gain_row):
    w = x.shape[1]
    ri = lax.broadcasted_iota(jnp.int32, (w, w), 0)
    ci = lax.broadcasted_iota(jnp.int32, (w, w), 1)
    same_half = jnp.where((ri // DIFF_DH) == (ci // DIFF_DH), 1.0, 0.0).astype(bf16)
    x2 = x * x
    hi = x2.astype(bf16)
    lo = (x2 - hi.astype(f32)).astype(bf16)
    ss = (jnp.dot(hi, same_half, preferred_element_type=f32)
          + jnp.dot(lo, same_half, preferred_element_type=f32))
    return x * lax.rsqrt(ss * (1.0 / DIFF_DH) + EPS) * gain_row


def _attn_kernel(q_ref, k_ref, v_ref, db_ref, qn_ref, kn_ref, lam_ref, subln_ref, o_ref,
                 k_s, vt_s, q_s, m_s, acc_s, *, lambda_init):
    tq = q_ref.shape[1]
    tk = db_ref.shape[2]
    T = k_ref.shape[1]
    dv = v_ref.shape[2]
    qi = pl.program_id(2)

    @pl.when(qi == 0)
    def _():
        def prep(i, carry):
            tiles = [i * ATTN_PREP_TILES + u for u in range(ATTN_PREP_TILES)]
            rows = [pl.ds(pl.multiple_of(j * tk, tk), tk) for j in tiles]
            kn = [_half_rmsnorm(k_ref[0, r, :], kn_ref[...]) for r in rows]
            vt = [v_ref[0, r, :].T for r in rows]
            for j, k, v in zip(tiles, kn, vt):
                k_s[j] = k.astype(bf16)
                vt_s[j] = v.astype(bf16)
            return carry
        lax.fori_loop(0, T // (tk * ATTN_PREP_TILES), prep, 0)

    qt = (_half_rmsnorm(q_ref[0], qn_ref[...]) * (DIFF_DH ** -0.5 * LOG2E)).T
    dim = lax.broadcasted_iota(jnp.int32, qt.shape, 0)
    q_s[0] = jnp.where(dim < DIFF_DH, qt, 0.0).astype(bf16)
    q_s[1] = jnp.where(dim < DIFF_DH, 0.0, qt).astype(bf16)

    r = tq // tk
    n_near = db_ref.shape[0]
    near_tiles = [r * (qi + 1) - 1 - t for t in range(n_near)]
    near_bias = [db_ref[t, 0] for t in range(n_near)]
    n_far = jnp.maximum(r * qi - 1, 0)
    n_grouped = (n_far // ATTN_GROUP) * ATTN_GROUP

    def unshifted(js, biases, acc, den):
        units = [(t, c) for t in range(len(js)) for c in range(2)]
        scores = {}
        for i in range(len(units) + ATTN_LOOKAHEAD):
            if i < len(units):
                t, c = units[i]
                s = jnp.dot(k_s[js[t]], q_s[c], preferred_element_type=f32)
                scores[i] = s if biases[t] is None else s + biases[t][:, c * tq:(c + 1) * tq]
            if i >= ATTN_LOOKAHEAD:
                t, c = units[i - ATTN_LOOKAHEAD]
                e = jnp.exp2(scores.pop(i - ATTN_LOOKAHEAD))
                pv = jnp.dot(vt_s[js[t]], e.astype(bf16), preferred_element_type=f32)
                esum = jnp.sum(e, axis=0, keepdims=True)
                acc[c] = pv if acc[c] is None else acc[c] + pv
                den[c] = esum if den[c] is None else den[c] + esum
        for c in range(2):
            acc_s[c, :dv, :] = acc[c]
            acc_s[c, dv:dv + 1, :] = den[c]

    @pl.when(qi == 0)
    def _():
        unshifted(near_tiles[:r], near_bias[:r], [None, None], [None, None])

    for rem in range(ATTN_GROUP):
        @pl.when(jnp.logical_and(qi >= 1, n_far - n_grouped == rem))
        def _():
            unshifted(near_tiles + [n_grouped + t for t in range(rem)],
                      near_bias + [None] * rem, [None, None], [None, None])

    def far_group(i, carry):
        unshifted([ATTN_GROUP * i + t for t in range(ATTN_GROUP)], [None] * ATTN_GROUP,
                  [acc_s[0, :dv, :], acc_s[1, :dv, :]], [acc_s[0, dv:dv + 1, :], acc_s[1, dv:dv + 1, :]])
        return carry
    lax.fori_loop(0, n_far // ATTN_GROUP, far_group, 0)

    denom = jnp.concatenate([acc_s[0, dv:dv + 1, :], acc_s[1, dv:dv + 1, :]], axis=1)
    trusted = jnp.logical_and(jnp.min(denom) >= SOFTMAX_DENOM_MIN, jnp.max(denom) <= SOFTMAX_DENOM_MAX)

    @pl.when(jnp.logical_not(trusted))
    def _():
        m_s[...] = jnp.full_like(m_s, NEG_BIG)
        acc_s[...] = jnp.zeros_like(acc_s)

        def online(js, biases):
            for c in range(2):
                ss = []
                for j, bias in zip(js, biases):
                    s = jnp.dot(k_s[j], q_s[c], preferred_element_type=f32)
                    ss.append(s if bias is None else s + bias[:, c * tq:(c + 1) * tq])
                m_old = m_s[c]
                m_new = m_old
                for s in ss:
                    m_new = jnp.maximum(m_new, jnp.max(s, axis=0, keepdims=True))
                alpha = jnp.exp2(m_old - m_new)
                acc = alpha * acc_s[c, :dv, :]
                den = alpha * acc_s[c, dv:dv + 1, :]
                for j, s in zip(js, ss):
                    e = jnp.exp2(s - m_new)
                    acc = acc + jnp.dot(vt_s[j], e.astype(bf16), preferred_element_type=f32)
                    den = den + jnp.sum(e, axis=0, keepdims=True)
                acc_s[c, :dv, :] = acc
                acc_s[c, dv:dv + 1, :] = den
                m_s[c] = m_new

        for t in range(r):
            online([near_tiles[t]], [near_bias[t]])

        @pl.when(qi >= 1)
        def _():
            online([near_tiles[r]], [near_bias[r]])

        def far_tile(j, carry):
            online([j], [None])
            return carry
        lax.fori_loop(0, n_far, far_tile, 0)

    lam_rows = lam_ref[...]
    lam = (jnp.exp(jnp.sum(lam_rows[0:1] * lam_rows[1:2], axis=-1, keepdims=True))
           - jnp.exp(jnp.sum(lam_rows[2:3] * lam_rows[3:4], axis=-1, keepdims=True)) + lambda_init)
    acc1 = acc_s[0]
    acc2 = acc_s[1]
    ot = acc1[:dv] * (1.0 / acc1[dv:dv + 1]) - lam * (acc2[:dv] * (1.0 / acc2[dv:dv + 1]))
    ont = ot * lax.rsqrt(jnp.mean(ot * ot, axis=0, keepdims=True) + EPS)
    o_ref[0] = (ont.T * (subln_ref[...] * (1.0 - lambda_init))).astype(o_ref.dtype)


def _diff_attention(proj, db, q_norm, k_norm, lq1, lk1, lq2, lk2, subln, lambda_init):
    B, T, _ = proj.shape
    tq, tk = ATTN_TQ, ATTN_TK
    two = lambda v: jnp.concatenate([v, v]).reshape(1, HEAD_W).astype(f32)
    lam_rows = jnp.zeros((8, DIFF_DH), f32).at[:4].set(jnp.stack([lq1, lk1, lq2, lk2]).astype(f32))
    qb, kb, vb = COL_DQ // HEAD_W, COL_DK // HEAD_W, COL_DV // HEAD_W
    return pl.pallas_call(
        functools.partial(_attn_kernel, lambda_init=lambda_init),
        out_shape=jax.ShapeDtypeStruct((B, T, GROUP_W), bf16),
        grid=(B, DIFF_HEADS, T // tq),
        in_specs=[
            pl.BlockSpec((1, tq, HEAD_W), lambda b, h, i: (b, i, qb + h)),
            pl.BlockSpec((1, T, HEAD_W), lambda b, h, i: (b, 0, kb + h)),
            pl.BlockSpec((1, T, HEAD_W), lambda b, h, i: (b, 0, vb + h)),
            pl.BlockSpec((tq // tk + 1, 1, tk, 2 * tq), lambda b, h, i: (0, h, 0, 0)),
            _const_spec((1, HEAD_W)),
            _const_spec((1, HEAD_W)),
            _const_spec((8, DIFF_DH)),
            _const_spec((1, HEAD_W)),
        ],
        out_specs=pl.BlockSpec((1, tq, HEAD_W), lambda b, h, i: (b, i, h)),
        scratch_shapes=[
            pltpu.VMEM((T // tk, tk, HEAD_W), bf16),
            pltpu.VMEM((T // tk, HEAD_W, tk), bf16),
            pltpu.VMEM((2, HEAD_W, tq), bf16),
            pltpu.VMEM((2, 1, tq), f32),
            pltpu.VMEM((2, HEAD_W + BF16_ROWS, tq), f32),
        ],
        compiler_params=_params(("arbitrary", "arbitrary", "arbitrary")),
        name="diff_attention",
    )(proj, proj, proj, db, two(q_norm), two(k_norm), lam_rows, subln.reshape(1, HEAD_W).astype(f32))


def _outffn_kernel(x_ref, yg_ref, yd_ref, mod_ref, g_ref, wout_ref, wup_ref, cw_ref, wdown_ref, o_ref,
                   h_s, acc_s, carry_s):
    nchunks = wup_ref.shape[0]
    fc = wdown_ref.shape[1]

    @pl.when(pl.program_id(1) == 0)
    def _():
        carry_s[...] = jnp.zeros_like(carry_s)

    half = yg_ref.shape[2]
    tm = h_s.shape[0]
    tr = tm // FFN_ROW_SPLIT
    acc_s[...] = jnp.zeros_like(acc_s)

    ys = [jnp.dot(yg_ref[0, r0:r0 + tr, :], wout_ref[:half, :], preferred_element_type=f32)
          + jnp.dot(yd_ref[0, r0:r0 + tr, :], wout_ref[half:, :], preferred_element_type=f32)
          for r0 in range(0, tm, tr)]
    us = []
    for r0, y in zip(range(0, tm, tr), ys):
        x1 = x_ref[0, r0:r0 + tr, :] + mod_ref[0, 2:3, :] * y
        o_ref[0, r0:r0 + tr, :] = x1
        h_s[r0:r0 + tr, :] = _modulated_norm(x1, g_ref[...], mod_ref, 3, 4).astype(bf16)
        us.append(jnp.dot(h_s[r0:r0 + tr, :], wup_ref[0], preferred_element_type=f32))

    def up(j):
        return [jnp.dot(h_s[r0:r0 + tr, :], wup_ref[j], preferred_element_type=f32)
                for r0 in range(0, tm, tr)]

    held = [[] for _ in range(FFN_ROW_SPLIT)]
    for j in range(nchunks):
        us_next = up(j + 1) if j + 1 < nchunks else None
        cw = cw_ref.at[j]
        prev8 = carry_s[j]
        flush = (j + 1) % FFN_DOWN_GROUP == 0 or j + 1 == nchunks
        for i, u in enumerate(us):
            yc = _causal_conv(u, prev8, cw, FFN_CONV) + cw[FFN_CONV:FFN_CONV + 1, :]
            prev8 = u[tr - 8:, :]
            held[i].append((_silu(yc[:, fc:]) * yc[:, :fc]).astype(bf16))
            if flush:
                j0 = j + 1 - len(held[i])
                w_down = jnp.concatenate([wdown_ref[jj] for jj in range(j0, j + 1)], axis=0)
                acc_s[i * tr:(i + 1) * tr, :] += jnp.dot(jnp.concatenate(held[i], axis=1), w_down,
                                                         preferred_element_type=f32)
                held[i] = []
        carry_s[j] = prev8
        us = us_next
    o_ref[0] = o_ref[0] + mod_ref[0, 5:6, :] * acc_s[...]


def _out_ffn(x, y_gdn, y_diff, mod_l, gain, w_out, w_up_c, conv_c, w_down_c):
    B, T, D = x.shape
    tm = TOKEN_TILE
    nchunks, _, fc2 = w_up_c.shape
    fc = fc2 // 2
    return pl.pallas_call(
        _outffn_kernel,
        out_shape=jax.ShapeDtypeStruct((B, T, D), f32),
        grid=(B, T // tm),
        in_specs=[
            pl.BlockSpec((1, tm, D), lambda b, t: (b, t, 0)),
            pl.BlockSpec((1, tm, GROUP_W), lambda b, t: (b, t, 0)),
            pl.BlockSpec((1, tm, GROUP_W), lambda b, t: (b, t, 0)),
            pl.BlockSpec((1, 6, D), lambda b, t: (b, 0, 0)),
            _const_spec((1, D)),
            _const_spec((2 * GROUP_W, D)),
            _const_spec((nchunks, D, fc2)),
            _const_spec((nchunks, 8, fc2)),
            _const_spec((nchunks, fc, D)),
        ],
        out_specs=pl.BlockSpec((1, tm, D), lambda b, t: (b, t, 0)),
        scratch_shapes=[
            pltpu.VMEM((tm, D), bf16),
            pltpu.VMEM((tm, D), f32),
            pltpu.VMEM((nchunks, 8, fc2), f32),
        ],
        compiler_params=_params(("arbitrary", "arbitrary")),
        name="out_ffn",
    )(x, y_gdn, y_diff, mod_l, gain.reshape(1, D), w_out, w_up_c, conv_c, w_down_c)


def _regroup_w_in(w_in_l):
    D = w_in_l.shape[0]
    o_b = 4 * GROUP_W
    o_d = o_b + 2 * GDN_HEADS
    small = jnp.zeros((D, HEAD_W), w_in_l.dtype).at[:, :2 * GDN_HEADS].set(w_in_l[:, o_b:o_d])
    return jnp.concatenate([w_in_l[:, :o_b], w_in_l[:, o_d:], small], axis=1).astype(bf16)


def _chunk_ffn(ffn_up_l, conv_w_l, conv_b_l, ffn_down_l, fc):
    D, two_f = ffn_up_l.shape
    F = two_f // 2
    n = F // fc
    pair = lambda m: jnp.concatenate([m[..., :F].reshape(m.shape[:-1] + (n, fc)),
                                      m[..., F:].reshape(m.shape[:-1] + (n, fc))], axis=-1)
    w_up_c = jnp.moveaxis(pair(ffn_up_l), -2, 0).astype(bf16)
    conv_rows = jnp.concatenate([conv_w_l.astype(f32), conv_b_l.astype(f32)[None]], axis=0)
    conv_c = jnp.moveaxis(pair(conv_rows), -2, 0)
    conv_c = jnp.zeros((n, 8, 2 * fc), f32).at[:, :FFN_CONV + 1].set(conv_c)
    w_down_c = ffn_down_l.reshape(n, fc, ffn_down_l.shape[1]).astype(bf16)
    return w_up_c, conv_c, w_down_c


def kernel(x, c, w_ada, b_ada, norm_mix, norm_ffn, w_in, gdn_conv_w, gdn_a_log, gdn_dt_bias, gdn_out_norm,
           diff_q_norm, diff_k_norm, diff_lambda_q1, diff_lambda_k1, diff_lambda_q2, diff_lambda_k2, diff_subln,
           rel_bias, w_out, ffn_up, ffn_conv_w, ffn_conv_b, ffn_down):
    B, T, D = x.shape
    depth = w_in.shape[0]
    mod = _ada_modulation(c, w_ada, b_ada).reshape(depth, B, 6, D)
    db = _bias_tiles(rel_bias, ATTN_TQ, ATTN_TK)
    for l in range(depth):
        proj = _in_projection(x, mod[l], norm_mix[l], _regroup_w_in(w_in[l]))
        y_gdn = _gdn_mixer(proj, gdn_conv_w[l], gdn_a_log[l], gdn_dt_bias[l], gdn_out_norm[l])
        lambda_init = 0.8 - 0.6 * math.exp(-0.3 * l)
        y_diff = _diff_attention(proj, db, diff_q_norm[l], diff_k_norm[l], diff_lambda_q1[l], diff_lambda_k1[l],
                                 diff_lambda_q2[l], diff_lambda_k2[l], diff_subln[l], lambda_init)
        w_up_c, conv_c, w_down_c = _chunk_ffn(ffn_up[l], ffn_conv_w[l], ffn_conv_b[l], ffn_down[l], FFN_CHUNK)
        x = _out_ffn(x, y_gdn, y_diff, mod[l], norm_ffn[l], w_out[l].astype(bf16), w_up_c, conv_c, w_down_c)
    return x
```
